```python
import math
import jax, jax.numpy as jnp
from jax import lax
import numpy as np

D_MODEL = 1024
BATCH = 32
SEQ = 2048
DEPTH = 2

GRID_W = 64
CTX_LEN = 256
Q_BLOCK = 128
EPS = 1e-6
ROPE_BASE = 10000.0
N_MOD = 6

SC_WIDTH = D_MODEL // 2
SC_IN = 3 * SC_WIDTH
SC_KERNEL = 3
MLA_V = 128
MLA_NOPE = 128
MLA_ROPE = 64
MLA_HEADS = (D_MODEL - SC_WIDTH) // MLA_V
MLA_Q_RANK = 3 * D_MODEL // 8
MLA_KV_RANK = D_MODEL // 4
EVEN_IN = SC_IN + MLA_Q_RANK + MLA_KV_RANK + MLA_ROPE
MLA_SCALE = (MLA_NOPE + MLA_ROPE) ** -0.5
DIFF_WIDTH = 3 * D_MODEL // 4
DIFF_HEAD = 64
DIFF_HEADS = DIFF_WIDTH // (2 * DIFF_HEAD)
DIFF_SCALE = DIFF_HEAD ** -0.5
FNET_WIDTH = D_MODEL - DIFF_WIDTH
FNET_GROUPS = 4
FNET_GROUP_DIM = FNET_WIDTH // FNET_GROUPS
ODD_IN = 3 * DIFF_WIDTH + FNET_WIDTH
D_FF = ((8 * D_MODEL // 3 + 127) // 128) * 128
N_EXPERTS = 8
TOP_K = 2

kernel_name = 'hybrid_diffusion_conv_mla_diffattn_fnet_moe'


def _rms_norm(x, g):
    xf = x.astype(jnp.float32)
    y = xf * lax.rsqrt(jnp.mean(xf * xf, axis=-1, keepdims=True) + EPS)
    return (y * g.astype(jnp.float32)).astype(x.dtype)


def _ada(cond, w, b):
    m = (jax.nn.silu(cond) @ w + b)[:, None, :]
    return jnp.split(m, N_MOD, axis=-1)


def _modulate(h, shift, scale):
    return h * (1 + scale) + shift


def _axial_rope(rows, rot_dim):
    row = jnp.repeat(jnp.arange(rows, dtype=jnp.float32), GRID_W)
    col = jnp.tile(jnp.arange(GRID_W, dtype=jnp.float32), rows)
    n_freq = rot_dim // 4
    inv = ROPE_BASE ** (-jnp.arange(n_freq, dtype=jnp.float32) / n_freq)
    ang = jnp.concatenate([row[:, None] * inv, col[:, None] * inv], axis=-1)
    return jnp.cos(ang), jnp.sin(ang)


def _apply_rope(x, rope):
    cos, sin = rope
    shape = (1, x.shape[1]) + (1,) * (x.ndim - 3) + (cos.shape[-1],)
    cos, sin = cos.reshape(shape), sin.reshape(shape)
    x1, x2 = jnp.split(x.astype(jnp.float32), 2, axis=-1)
    return jnp.concatenate([x1 * cos - x2 * sin, x1 * sin + x2 * cos], axis=-1).astype(x.dtype)


def _map_query_blocks(fn, q):
    B, T = q.shape[:2]
    nb = T // Q_BLOCK
    qb = jnp.moveaxis(q.reshape((B, nb, Q_BLOCK) + q.shape[2:]), 1, 0)
    out = lax.map(fn, qb)
    return jnp.moveaxis(out, 0, 1).reshape((B, T) + out.shape[3:])


def _swiglu(h, wg, wu, wd):
    return (jax.nn.silu(h @ wg) * (h @ wu)) @ wd


def _moe(h, router_w, w_gate, w_up, w_down):
    logits = (h @ router_w).astype(jnp.float32)
    top_v, top_i = lax.top_k(logits, TOP_K)
    top_w = jax.nn.softmax(top_v, axis=-1)
    gates = jnp.einsum('btk,btke->bte', top_w, jax.nn.one_hot(top_i, N_EXPERTS, dtype=jnp.float32)).astype(h.dtype)
    out = jnp.zeros_like(h)
    for e in range(N_EXPERTS):
        out = out + gates[..., e:e + 1] * _swiglu(h, w_gate[e], w_up[e], w_down[e])
    return out


def _short_conv(z, conv_w):
    b_gate, c_gate, u = jnp.split(z, 3, axis=-1)
    u = c_gate * u
    conv = lax.conv_general_dilated(u, conv_w[:, None, :], window_strides=(1,), padding=((SC_KERNEL // 2, SC_KERNEL // 2),),
                                    dimension_numbers=('NWC', 'WIO', 'NWC'), feature_group_count=SC_WIDTH)
    return b_gate * conv


def _mla_q(zq, q_norm_g, w_uq, rope):
    B, T = zq.shape[:2]
    q = (_rms_norm(zq, q_norm_g) @ w_uq).reshape(B, T, MLA_HEADS, MLA_NOPE + MLA_ROPE)
    q_nope, q_pe = q[..., :MLA_NOPE], q[..., MLA_NOPE:]
    if rope is not None:
        q_pe = _apply_rope(q_pe, rope)
    return jnp.concatenate([q_nope, q_pe], axis=-1)


def _mla_kv(zkv, kv_norm_g, w_ukv, rope):
    B, T = zkv.shape[:2]
    kv = (_rms_norm(zkv[..., :MLA_KV_RANK], kv_norm_g) @ w_ukv).reshape(B, T, MLA_HEADS, MLA_NOPE + MLA_V)
    k_nope, v = kv[..., :MLA_NOPE], kv[..., MLA_NOPE:]
    k_pe = zkv[..., MLA_KV_RANK:][:, :, None, :]
    if rope is not None:
        k_pe = _apply_rope(k_pe, rope)
    k = jnp.concatenate([k_nope, jnp.broadcast_to(k_pe, (B, T, MLA_HEADS, MLA_ROPE))], axis=-1)
    return k, v


def _mla_block(qb, k, v):
    s = jnp.einsum('bqhd,bkhd->bhqk', qb, k).astype(jnp.float32) * MLA_SCALE
    p = jax.nn.softmax(s, axis=-1).astype(v.dtype)
    return jnp.einsum('bhqk,bkhd->bqhd', p, v)


def _even_mixer(h, hc, rope, w_in, conv_w, q_norm_g, w_uq, kv_norm_g, w_ukv, w_out, need_ctx):
    B, T, _ = h.shape
    z = h @ w_in
    a_lat = _short_conv(z[..., :SC_IN], conv_w)
    q_lat = _mla_q(z[..., SC_IN:SC_IN + MLA_Q_RANK], q_norm_g, w_uq, rope)
    k_lat, v_lat = _mla_kv(z[..., SC_IN + MLA_Q_RANK:], kv_norm_g, w_ukv, rope)
    if need_ctx:
        zc = hc @ w_in
        k_ctx, v_ctx = _mla_kv(zc[..., SC_IN + MLA_Q_RANK:], kv_norm_g, w_ukv, None)
    else:
        k_ctx, v_ctx = _mla_kv(hc @ w_in[:, SC_IN + MLA_Q_RANK:], kv_norm_g, w_ukv, None)
    k_all = jnp.concatenate([k_ctx, k_lat], axis=1)
    v_all = jnp.concatenate([v_ctx, v_lat], axis=1)
    b_lat = _map_query_blocks(lambda qb: _mla_block(qb, k_all, v_all), q_lat)
    y = jnp.concatenate([a_lat, b_lat.reshape(B, T, -1)], axis=-1) @ w_out
    yc = None
    if need_ctx:
        L = hc.shape[1]
        a_ctx = _short_conv(zc[..., :SC_IN], conv_w)
        q_ctx = _mla_q(zc[..., SC_IN:SC_IN + MLA_Q_RANK], q_norm_g, w_uq, None)
        b_ctx = _map_query_blocks(lambda qb: _mla_block(qb, k_ctx, v_ctx), q_ctx)
        yc = jnp.concatenate([a_ctx, b_ctx.reshape(hc.shape[0], L, -1)], axis=-1) @ w_out
    return y, yc


def _diff_q(zq, rope):
    B, T = zq.shape[:2]
    q = zq.reshape(B, T, DIFF_HEADS, 2, DIFF_HEAD)
    return q if rope is None else _apply_rope(q, rope)


def _diff_kv(zkv, rope):
    B, T = zkv.shape[:2]
    k = zkv[..., :DIFF_WIDTH].reshape(B, T, DIFF_HEADS, 2, DIFF_HEAD)
    if rope is not None:
        k = _apply_rope(k, rope)
    v = zkv[..., DIFF_WIDTH:].reshape(B, T, DIFF_HEADS, 2 * DIFF_HEAD)
    return k, v


def _diff_block(qb, k, v, lam, subln_g, lam_init):
    s = jnp.einsum('bqhmd,bkhmd->bhmqk', qb, k).astype(jnp.float32) * DIFF_SCALE
    p = jax.nn.softmax(s, axis=-1)
    a = (p[:, :, 0] - lam * p[:, :, 1]).astype(v.dtype)
    o = jnp.einsum('bhqk,bkhe->bqhe', a, v)
    return _rms_norm(o, subln_g) * (1.0 - lam_init)


def _fourier(f):
    B, T, _ = f.shape
    g = f.reshape(B, T, FNET_GROUPS, FNET_GROUP_DIM).astype(jnp.float32)
    y = jnp.fft.fftn(g, axes=(1, 3), norm='ortho').real
    return y.reshape(B, T, FNET_WIDTH).astype(f.dtype)


def _odd_mixer(h, hc, rope, w_in, lam_p, subln_g, w_out, lam_init, need_ctx):
    B, T, _ = h.shape
    lp = lam_p.astype(jnp.float32)
    lam = jnp.exp(jnp.sum(lp[0] * lp[1])) - jnp.exp(jnp.sum(lp[2] * lp[3])) + lam_init
    z = h @ w_in
    q_lat = _diff_q(z[..., :DIFF_WIDTH], rope)
    k_lat, v_lat = _diff_kv(z[..., DIFF_WIDTH:3 * DIFF_WIDTH], rope)
    if need_ctx:
        zc = hc @ w_in
        k_ctx, v_ctx = _diff_kv(zc[..., DIFF_WIDTH:3 * DIFF_WIDTH], None)
    else:
        k_ctx, v_ctx = _diff_kv(hc @ w_in[:, DIFF_WIDTH:3 * DIFF_WIDTH], None)
    k_all = jnp.concatenate([k_ctx, k_lat], axis=1)
    v_all = jnp.concatenate([v_ctx, v_lat], axis=1)
    c_lat = _map_query_blocks(lambda qb: _diff_block(qb, k_all, v_all, lam, subln_g, lam_init), q_lat)
    d_lat = _fourier(z[..., 3 * DIFF_WIDTH:])
    y = jnp.concatenate([c_lat.reshape(B, T, DIFF_WIDTH), d_lat], axis=-1) @ w_out
    yc = None
    if need_ctx:
        L = hc.shape[1]
        q_ctx = _diff_q(zc[..., :DIFF_WIDTH], None)
        c_ctx_out = _map_query_blocks(lambda qb: _diff_block(qb, k_ctx, v_ctx, lam, subln_g, lam_init), q_ctx)
        d_ctx = _fourier(zc[..., 3 * DIFF_WIDTH:])
        yc = jnp.concatenate([c_ctx_out.reshape(hc.shape[0], L, DIFF_WIDTH), d_ctx], axis=-1) @ w_out
    return y, yc


def setup_inputs(seed: int = 0) -> dict:
    key = jax.random.key(seed)
    ks = iter(list(jax.random.split(key, 32)))
    ne, no = (DEPTH + 1) // 2, DEPTH // 2
    D, F, E = D_MODEL, D_FF, N_EXPERTS

    def nrm(shape, std):
        return std * jax.random.normal(next(ks), shape, jnp.float32)

    def gain(shape):
        return 1.0 + nrm(shape, 0.02)

    return {
        'x': nrm((BATCH, SEQ, D), 1.0),
        'c': nrm((BATCH, D), 1.0),
        'ctx': nrm((BATCH, CTX_LEN, D), 1.0),
        'c_ctx': nrm((D,), 1.0),
        'ev_mod_w': nrm((ne, D, N_MOD * D), 0.5 * D ** -0.5),
        'ev_mod_b': nrm((ne, N_MOD * D), 0.02),
        'ev_norm_g': gain((ne, 4, D)),
        'ev_w_in': nrm((ne, D, EVEN_IN), D ** -0.5),
        'ev_conv_w': nrm((ne, SC_KERNEL, SC_WIDTH), SC_KERNEL ** -0.5),
        'ev_q_norm_g': gain((ne, MLA_Q_RANK)),
        'ev_w_uq': nrm((ne, MLA_Q_RANK, MLA_HEADS * (MLA_NOPE + MLA_ROPE)), MLA_Q_RANK ** -0.5),
        'ev_kv_norm_g': gain((ne, MLA_KV_RANK)),
        'ev_w_ukv': nrm((ne, MLA_KV_RANK, MLA_HEADS * (MLA_NOPE + MLA_V)), MLA_KV_RANK ** -0.5),
        'ev_w_out': nrm((ne, D, D), D ** -0.5),
        'ev_ffn_gate': nrm((ne, D, F), D ** -0.5),
        'ev_ffn_up': nrm((ne, D, F), D ** -0.5),
        'ev_ffn_down': nrm((ne, F, D), F ** -0.5),
        'od_mod_w': nrm((no, D, N_MOD * D), 0.5 * D ** -0.5),
        'od_mod_b': nrm((no, N_MOD * D), 0.02),
        'od_norm_g': gain((no, 4, D)),
        'od_w_in': nrm((no, D, ODD_IN), D ** -0.5),
        'od_lambda': nrm((no, 4, DIFF_HEAD), 0.1),
        'od_subln_g': gain((no, 2 * DIFF_HEAD)),
        'od_w_out': nrm((no, D, D), D ** -0.5),
        'od_router': nrm((no, D, E), D ** -0.5),
        'od_exp_gate': nrm((no, E, D, F), D ** -0.5),
        'od_exp_up': nrm((no, E, D, F), D ** -0.5),
        'od_exp_down': nrm((no, E, F, D), F ** -0.5),
    }


def reference(x, c, ctx, c_ctx, ev_mod_w, ev_mod_b, ev_norm_g, ev_w_in, ev_conv_w, ev_q_norm_g, ev_w_uq,
              ev_kv_norm_g, ev_w_ukv, ev_w_out, ev_ffn_gate, ev_ffn_up, ev_ffn_down, od_mod_w, od_mod_b,
              od_norm_g, od_w_in, od_lambda, od_subln_g, od_w_out, od_router, od_exp_gate, od_exp_up, od_exp_down):
    rows = x.shape[1] // GRID_W
    rope_mla = _axial_rope(rows, MLA_ROPE)
    rope_diff = _axial_rope(rows, DIFF_HEAD)
    c_ctx_row = c_ctx[None, :]
    for i in range(DEPTH):
        j = i // 2
        even = i % 2 == 0
        need_ctx = i < DEPTH - 1
        if even:
            mod_w, mod_b, norm_g = ev_mod_w[j], ev_mod_b[j], ev_norm_g[j]
            ffn = lambda t: _swiglu(t, ev_ffn_gate[j], ev_ffn_up[j], ev_ffn_down[j])
        else:
            mod_w, mod_b, norm_g = od_mod_w[j], od_mod_b[j], od_norm_g[j]
            ffn = lambda t: _moe(t, od_router[j], od_exp_gate[j], od_exp_up[j], od_exp_down[j])
        sh1, sc1, g1, sh2, sc2, g2 = _ada(c, mod_w, mod_b)
        csh1, csc1, cg1, csh2, csc2, cg2 = _ada(c_ctx_row, mod_w, mod_b)
        h = _modulate(_rms_norm(x, norm_g[0]), sh1, sc1)
        hc = _modulate(_rms_norm(ctx, norm_g[0]), csh1, csc1)
        if even:
            y, yc = _even_mixer(h, hc, rope_mla, ev_w_in[j], ev_conv_w[j], ev_q_norm_g[j], ev_w_uq[j],
                                ev_kv_norm_g[j], ev_w_ukv[j], ev_w_out[j], need_ctx)
        else:
            lam_init = 0.8 - 0.6 * math.exp(-0.3 * i)
            y, yc = _odd_mixer(h, hc, rope_diff, od_w_in[j], od_lambda[j], od_subln_g[j], od_w_out[j],
                               lam_init, need_ctx)
        x = x + g1 * _rms_norm(y, norm_g[1])
        h2 = _modulate(_rms_norm(x, norm_g[2]), sh2, sc2)
        x = x + g2 * _rms_norm(ffn(h2), norm_g[3])
        if need_ctx:
            ctx = ctx + cg1 * _rms_norm(yc, norm_g[1])
            hc2 = _modulate(_rms_norm(ctx, norm_g[2]), csh2, csc2)
            ctx = ctx + cg2 * _rms_norm(ffn(hc2), norm_g[3])
    return x
```

```python
import functools
import math

import numpy as np
import jax
import jax.numpy as jnp
from jax import lax
from jax.experimental import pallas as pl
from jax.experimental.pallas import tpu as pltpu

F32 = jnp.float32
BF16 = jnp.bfloat16
I32 = jnp.int32
U32 = jnp.uint32

D = 1024
GRID_W = 64
EPS = 1e-6
ROPE_BASE = 10000.0
N_MOD = 6
SC_W = D // 2
MLA_V = 128
MLA_NOPE = 128
MLA_ROPE = 64
MLA_H = (D - SC_W) // MLA_V
MLA_QR = 3 * D // 8
MLA_KVR = D // 4
MLA_SCALE = (MLA_NOPE + MLA_ROPE) ** -0.5
DIFF_W = 3 * D // 4
DIFF_HD = 64
DIFF_H = DIFF_W // (2 * DIFF_HD)
DIFF_SCALE = DIFF_HD ** -0.5
FNET_W = D - DIFF_W
FNET_G = 4
FNET_GD = FNET_W // FNET_G
D_FF = ((8 * D // 3 + 127) // 128) * 128
N_EXP = 8
TOP_K = 2
LOG2E = math.log2(math.e)

LANE = 128
VMEM_LIMIT = 56 * 1024 * 1024

TM = 512
TQ_MLA = 512
TQ_DIFF = 256
TK_FFT = 512
TM_EXP = 512
TS_MOE = 256


def _cp(sem, vmem=VMEM_LIMIT):
    return pltpu.CompilerParams(dimension_semantics=sem, vmem_limit_bytes=vmem)


def _rms(x, g):
    return x * lax.rsqrt(jnp.mean(x * x, axis=-1, keepdims=True) + EPS) * g


def _dot(a, b):
    return jnp.dot(a, b, preferred_element_type=F32)


def _dot_nt(a, b):
    return lax.dot_general(a, b, (((1,), (1,)), ((), ())), preferred_element_type=F32)


def _full(shape):
    nd = len(shape)
    return pl.BlockSpec(shape, lambda *_: (0,) * nd)


def _pack_bf16_pairs(x):
    n = x.shape[1] // 2
    hi = pltpu.bitcast(x[:, :n].astype(BF16).astype(F32), U32)
    lo = pltpu.bitcast(x[:, n:].astype(BF16).astype(F32), U32)
    return hi | (lo >> 16)


def _unpack_bf16_pairs(u):
    hi = pltpu.bitcast(u & jnp.uint32(0xFFFF0000), F32)
    lo = pltpu.bitcast(u << 16, F32)
    return jnp.concatenate([hi, lo], axis=1)


def _mod_kernel(c_ref, w_ref, b_ref, o_ref):
    c = c_ref[...]
    s = c / (1.0 + jnp.exp(-c))
    o_ref[...] = _dot(s.astype(BF16), w_ref[...]) + b_ref[...]


def _modulation(cond, w_bf, b):
    rows = cond.shape[0]
    n = w_bf.shape[1]
    tn = 1536
    return pl.pallas_call(
        _mod_kernel,
        grid=(n // tn,),
        in_specs=[_full((rows, D)), pl.BlockSpec((D, tn), lambda j: (0, j)), pl.BlockSpec((1, tn), lambda j: (0, j))],
        out_specs=pl.BlockSpec((rows, tn), lambda j: (0, j)),
        out_shape=jax.ShapeDtypeStruct((rows, n), F32),
        compiler_params=_cp(("arbitrary",)),
        name="modulation",
    )(cond, w_bf, b.reshape(1, n))


def _front_even_kernel(x_ref, sh_ref, sc_ref, g0_ref, win_ref, qg_ref, wq_ref, wqs_ref, kvg_ref, wkv_ref,
                       cos_ref, sin_ref, bg_ref, uc_ref, q_ref, k_ref, v_ref):
    x = x_ref[...]
    h = _rms(x, g0_ref[...]) * (1.0 + sc_ref[0]) + sh_ref[0]
    z = _dot(h.astype(BF16), win_ref[...])
    bg_ref[...] = z[:, 0:SC_W].astype(BF16)
    uc_ref[...] = (z[:, SC_W:2 * SC_W] * z[:, 2 * SC_W:3 * SC_W]).astype(BF16)
    o = 3 * SC_W
    zq = z[:, o:o + MLA_QR]
    zkv = z[:, o + MLA_QR:o + MLA_QR + MLA_KVR]
    o2 = o + MLA_QR + MLA_KVR
    kpe = z[:, o2:o2 + LANE]
    kpes = z[:, o2 + LANE:o2 + 2 * LANE]
    cos = cos_ref[...]
    sin = sin_ref[...]
    zqn = _rms(zq, qg_ref[...]).astype(BF16)
    qm = _dot(zqn, wq_ref[...])
    qs = _dot(zqn, wqs_ref[...])
    qscale = MLA_SCALE * LOG2E
    for hd in range(MLA_H):
        lo = qm[:, 256 * hd:256 * hd + LANE]
        hi = qm[:, 256 * hd + LANE:256 * hd + 2 * LANE] * cos + qs[:, LANE * hd:LANE * hd + LANE] * sin
        q_ref[hd, :, 0:LANE] = (lo * qscale).astype(BF16)
        q_ref[hd, :, LANE:2 * LANE] = (hi * qscale).astype(BF16)
    zkvn = _rms(zkv, kvg_ref[...]).astype(BF16)
    kv = _dot(zkvn, wkv_ref[...])
    kpr = (kpe * cos + kpes * sin).astype(BF16)
    for hd in range(MLA_H):
        k_ref[hd, :, 0:LANE] = kv[:, 256 * hd:256 * hd + LANE].astype(BF16)
        k_ref[hd, :, LANE:2 * LANE] = kpr
        v_ref[hd] = kv[:, 256 * hd + LANE:256 * hd + 2 * LANE].astype(BF16)


def _front_even(x2d, sh, sc, row_of, g0, w_in, qg, wq, wqs, kvg, wkv, cos, sin, seq):
    n = x2d.shape[0]
    tm = min(TM, seq)
    nper = seq // tm
    rowspec = pl.BlockSpec((1, 1, D), lambda i: (row_of(i // nper), 0, 0))
    tabspec = pl.BlockSpec((tm, LANE), lambda i: (i % nper, 0))
    win_n = w_in.shape[1]
    return pl.pallas_call(
        _front_even_kernel,
        grid=(n // tm,),
        in_specs=[pl.BlockSpec((tm, D), lambda i: (i, 0)), rowspec, rowspec, _full((1, D)), _full((D, win_n)),
                  _full((1, MLA_QR)), _full((MLA_QR, 4 * 256)), _full((MLA_QR, 4 * LANE)),
                  _full((1, MLA_KVR)), _full((MLA_KVR, 4 * 256)), tabspec, tabspec],
        out_specs=[pl.BlockSpec((tm, SC_W), lambda i: (i, 0)), pl.BlockSpec((tm, SC_W), lambda i: (i, 0)),
                   pl.BlockSpec((MLA_H, tm, 256), lambda i: (0, i, 0)),
                   pl.BlockSpec((MLA_H, tm, 256), lambda i: (0, i, 0)),
                   pl.BlockSpec((MLA_H, tm, LANE), lambda i: (0, i, 0))],
        out_shape=[jax.ShapeDtypeStruct((n, SC_W), BF16), jax.ShapeDtypeStruct((n, SC_W), BF16),
                   jax.ShapeDtypeStruct((MLA_H, n, 256), BF16), jax.ShapeDtypeStruct((MLA_H, n, 256), BF16),
                   jax.ShapeDtypeStruct((MLA_H, n, LANE), BF16)],
        compiler_params=_cp(("parallel",)),
        name="front_even",
    )(x2d, sh, sc, g0, w_in, qg, wq, wqs, kvg, wkv, cos, sin)


def _mla_attn_kernel(*refs, with_lat):
    if with_lat:
        q_ref, kc_ref, vc_ref, kl_ref, vl_ref, o_ref = refs
    else:
        q_ref, kc_ref, vc_ref, o_ref = refs
    q = q_ref[0]
    s_c = _dot_nt(q, kc_ref[0])
    m = jnp.max(s_c, axis=-1, keepdims=True)
    if with_lat:
        s_l = _dot_nt(q, kl_ref[0])
        m = jnp.maximum(m, jnp.max(s_l, axis=-1, keepdims=True))
    p_c = jnp.exp2(s_c - m)
    l = jnp.sum(p_c, axis=-1, keepdims=True)
    o = _dot(p_c.astype(BF16), vc_ref[0])
    if with_lat:
        p_l = jnp.exp2(s_l - m)
        l = l + jnp.sum(p_l, axis=-1, keepdims=True)
        o = o + _dot(p_l.astype(BF16), vl_ref[0])
    o_ref[...] = (o * (1.0 / l)).astype(BF16)


def _mla_attn(q, k_ctx, v_ctx, k_lat, v_lat, nb, seq_q, ctx_len):
    with_lat = k_lat is not None
    tq = min(TQ_MLA, seq_q)
    nq = seq_q // tq
    in_specs = [pl.BlockSpec((1, tq, 256), lambda b, h, i: (h, b * nq + i, 0)),
                pl.BlockSpec((1, ctx_len, 256), lambda b, h, i: (h, b, 0)),
                pl.BlockSpec((1, ctx_len, LANE), lambda b, h, i: (h, b, 0))]
    args = [q, k_ctx, v_ctx]
    if with_lat:
        in_specs += [pl.BlockSpec((1, seq_q, 256), lambda b, h, i: (h, b, 0)),
                     pl.BlockSpec((1, seq_q, LANE), lambda b, h, i: (h, b, 0))]
        args += [k_lat, v_lat]
    return pl.pallas_call(
        functools.partial(_mla_attn_kernel, with_lat=with_lat),
        grid=(nb, MLA_H, nq),
        in_specs=in_specs,
        out_specs=pl.BlockSpec((tq, LANE), lambda b, h, i: (b * nq + i, h)),
        out_shape=jax.ShapeDtypeStruct((nb * seq_q, MLA_H * MLA_V), BF16),
        compiler_params=_cp(("parallel", "parallel", "arbitrary")),
        name="mla_attn_lat" if with_lat else "mla_attn_ctx",
    )(*args)


def _residual_and_h2(y, x_ref, g1_ref, n1_ref, n2_ref, sh2_ref, sc2_ref):
    x1 = x_ref[...] + g1_ref[0] * _rms(y, n1_ref[...])
    h2 = _rms(x1, n2_ref[...]) * (1.0 + sc2_ref[0]) + sh2_ref[0]
    return x1, h2


def _mix_even_kernel(bg_ref, uc_ref, ucp_ref, ucn_ref, cw_ref, at_ref, wo_ref, x_ref, g1_ref, n1_ref, n2_ref,
                     sh2_ref, sc2_ref, x1_ref, h2_ref, scr, *, nper):
    tm = uc_ref.shape[0]
    i = pl.program_id(0)
    ucf = uc_ref[...].astype(F32)
    first = (i % nper) == 0
    last = (i % nper) == nper - 1
    prev_row = jnp.where(first, 0.0, ucp_ref[7:8, :].astype(F32))
    next_row = jnp.where(last, 0.0, ucn_ref[0:1, :].astype(F32))
    scr[8:8 + tm, :] = ucf
    scr[7:8, :] = prev_row
    scr[8 + tm:9 + tm, :] = next_row
    up = scr[7:7 + tm, :]
    dn = scr[9:9 + tm, :]
    conv = cw_ref[0:1, :] * up + cw_ref[1:2, :] * ucf + cw_ref[2:3, :] * dn
    a = (bg_ref[...].astype(F32) * conv).astype(BF16)
    y = _dot(a, wo_ref[0:SC_W, :]) + _dot(at_ref[...], wo_ref[SC_W:D, :])
    x1, h2 = _residual_and_h2(y, x_ref, g1_ref, n1_ref, n2_ref, sh2_ref, sc2_ref)
    x1_ref[...] = x1
    h2_ref[...] = h2.astype(BF16)


def _mix_even(bg, uc, conv_w, attn, w_out, x2d, g1, n1, n2, sh2, sc2, row_of, seq):
    n = x2d.shape[0]
    tm = min(TM, seq)
    nper = seq // tm
    nb8 = n // 8
    rowspec = pl.BlockSpec((1, 1, D), lambda i: (row_of(i // nper), 0, 0))
    tile = lambda w: pl.BlockSpec((tm, w), lambda i: (i, 0))
    return pl.pallas_call(
        functools.partial(_mix_even_kernel, nper=nper),
        grid=(n // tm,),
        in_specs=[tile(SC_W), tile(SC_W),
                  pl.BlockSpec((8, SC_W), lambda i: (jnp.maximum(i * (tm // 8) - 1, 0), 0)),
                  pl.BlockSpec((8, SC_W), lambda i: (jnp.minimum((i + 1) * (tm // 8), nb8 - 1), 0)),
                  _full((3, SC_W)), tile(MLA_H * MLA_V), _full((D, D)), tile(D), rowspec, _full((1, D)),
                  _full((1, D)), rowspec, rowspec],
        out_specs=[tile(D), tile(D)],
        out_shape=[jax.ShapeDtypeStruct((n, D), F32), jax.ShapeDtypeStruct((n, D), BF16)],
        scratch_shapes=[pltpu.VMEM((tm + 16, SC_W), F32)],
        compiler_params=_cp(("parallel",)),
        name="mix_even",
    )(bg, uc, uc, uc, conv_w, attn, w_out, x2d, g1, n1, n2, sh2, sc2)


def _mix_odd_kernel(ca_ref, fd_ref, wo_ref, x_ref, g1_ref, n1_ref, n2_ref, sh2_ref, sc2_ref, rw_ref,
                    x1_ref, h2p_ref, ridx_ref, rwt_ref):
    y = _dot(ca_ref[...], wo_ref[0:DIFF_W, :]) + _dot(fd_ref[...], wo_ref[DIFF_W:D, :])
    x1, h2 = _residual_and_h2(y, x_ref, g1_ref, n1_ref, n2_ref, sh2_ref, sc2_ref)
    x1_ref[...] = x1
    h2p_ref[...] = _pack_bf16_pairs(h2)
    logits = jnp.dot(h2, rw_ref[...], preferred_element_type=F32, precision=lax.Precision.HIGHEST)
    lane = lax.broadcasted_iota(I32, logits.shape, 1).astype(F32)
    neg = jnp.float32(-jnp.inf)
    s0 = jnp.where(lane < N_EXP, logits, neg)
    m1 = jnp.max(s0, axis=-1, keepdims=True)
    i1 = jnp.min(jnp.where(s0 == m1, lane, float(LANE)), axis=-1, keepdims=True)
    s1 = jnp.where(lane == i1, neg, s0)
    m2 = jnp.max(s1, axis=-1, keepdims=True)
    i2 = jnp.min(jnp.where(s1 == m2, lane, float(LANE)), axis=-1, keepdims=True)
    e = jnp.exp(m2 - m1)
    w1 = 1.0 / (1.0 + e)
    w2 = e * w1
    ridx_ref[...] = jnp.where(lane == 0.0, i1, jnp.where(lane == 1.0, i2, 0.0)).astype(I32)
    rwt_ref[...] = jnp.where(lane == 0.0, w1, jnp.where(lane == 1.0, w2, 0.0))


def _mix_odd(cattn, fd, w_out, x2d, g1, n1, n2, sh2, sc2, router_pad, row_of, seq):
    n = x2d.shape[0]
    tm = min(TM, seq)
    nper = seq // tm
    rowspec = pl.BlockSpec((1, 1, D), lambda i: (row_of(i // nper), 0, 0))
    tile = lambda w: pl.BlockSpec((tm, w), lambda i: (i, 0))
    return pl.pallas_call(
        _mix_odd_kernel,
        grid=(n // tm,),
        in_specs=[tile(DIFF_W), tile(FNET_W), _full((D, D)), tile(D), rowspec, _full((1, D)), _full((1, D)),
                  rowspec, rowspec, _full((D, LANE))],
        out_specs=[tile(D), tile(D // 2), tile(LANE), tile(LANE)],
        out_shape=[jax.ShapeDtypeStruct((n, D), F32), jax.ShapeDtypeStruct((n, D // 2), U32),
                   jax.ShapeDtypeStruct((n, LANE), I32), jax.ShapeDtypeStruct((n, LANE), F32)],
        compiler_params=_cp(("parallel",)),
        name="mix_odd",
    )(cattn, fd, w_out, x2d, g1, n1, n2, sh2, sc2, router_pad)


def _swiglu(h, wg, wu, wd):
    g = _dot(h, wg)
    u = _dot(h, wu)
    a = (g / (1.0 + jnp.exp(-g)) * u).astype(BF16)
    return _dot(a, wd)


def _ffn_kernel(h_ref, wg_ref, wu_ref, wd_ref, x_ref, g2_ref, n3_ref, o_ref):
    f = _swiglu(h_ref[...], wg_ref[...], wu_ref[...], wd_ref[...])
    o_ref[...] = x_ref[...] + g2_ref[0] * _rms(f, n3_ref[...])


def _ffn_dense(h2, wg, wu, wd, x1, g2, n3, row_of, seq):
    n = x1.shape[0]
    tm = min(TM, seq)
    nper = seq // tm
    rowspec = pl.BlockSpec((1, 1, D), lambda i: (row_of(i // nper), 0, 0))
    tile = lambda w: pl.BlockSpec((tm, w), lambda i: (i, 0))
    once = lambda shape: pl.BlockSpec(shape, lambda i: (0, 0), pipeline_mode=pl.Buffered(1))
    return pl.pallas_call(
        _ffn_kernel,
        grid=(n // tm,),
        in_specs=[tile(D), once((D, D_FF)), once((D, D_FF)), once((D_FF, D)), tile(D), rowspec, _full((1, D))],
        out_specs=tile(D),
        out_shape=jax.ShapeDtypeStruct((n, D), F32),
        compiler_params=_cp(("parallel",)),
        name="ffn_dense",
    )(h2, wg, wu, wd, x1, g2, n3)


def _rope_slab(x, cos, sin_signed, lane):
    swap = jnp.where((lane & 63) < 32, pltpu.roll(x, 96, 1), pltpu.roll(x, 32, 1))
    return x * cos + swap * sin_signed


def _front_odd_kernel(x_ref, sh_ref, sc_ref, g0_ref, win_ref, cos_ref, sin_ref, q_ref, k_ref, v_ref, f_ref,
                      *, with_q):
    x = x_ref[...]
    h = _rms(x, g0_ref[...]) * (1.0 + sc_ref[0]) + sh_ref[0]
    z = _dot(h.astype(BF16), win_ref[...])
    cos = cos_ref[...]
    sin = sin_ref[...]
    lane = lax.broadcasted_iota(I32, cos.shape, 1)
    off = DIFF_W if with_q else 0
    qscale = DIFF_SCALE * LOG2E
    for g in range(DIFF_W // LANE):
        sl = slice(LANE * g, LANE * g + LANE)
        if with_q:
            q_ref[:, sl] = (_rope_slab(z[:, sl], cos, sin, lane) * qscale).astype(BF16)
        ksl = slice(off + LANE * g, off + LANE * g + LANE)
        k_ref[:, sl] = _rope_slab(z[:, ksl], cos, sin, lane).astype(BF16)
    v_ref[...] = z[:, off + DIFF_W:off + 2 * DIFF_W].astype(BF16)
    if with_q:
        f_ref[...] = z[:, 3 * DIFF_W:3 * DIFF_W + FNET_W].astype(BF16)


def _front_odd(x2d, sh, sc, row_of, g0, w_in, cos, sin, seq, with_q):
    n = x2d.shape[0]
    tm = min(TM, seq)
    nper = seq // tm
    rowspec = pl.BlockSpec((1, 1, D), lambda i: (row_of(i // nper), 0, 0))
    tabspec = pl.BlockSpec((tm, LANE), lambda i: (i % nper, 0))
    tile = lambda w: pl.BlockSpec((tm, w), lambda i: (i, 0))
    if with_q:
        kern = functools.partial(_front_odd_kernel, with_q=True)
        out_specs = [tile(DIFF_W), tile(DIFF_W), tile(DIFF_W), tile(FNET_W)]
        out_shape = [jax.ShapeDtypeStruct((n, DIFF_W), BF16)] * 3 + [jax.ShapeDtypeStruct((n, FNET_W), BF16)]
    else:
        def kern(x_ref, sh_ref, sc_ref, g0_ref, win_ref, cos_ref, sin_ref, k_ref, v_ref):
            _front_odd_kernel(x_ref, sh_ref, sc_ref, g0_ref, win_ref, cos_ref, sin_ref, None, k_ref, v_ref, None,
                              with_q=False)
        out_specs = [tile(DIFF_W), tile(DIFF_W)]
        out_shape = [jax.ShapeDtypeStruct((n, DIFF_W), BF16)] * 2
    return pl.pallas_call(
        kern,
        grid=(n // tm,),
        in_specs=[tile(D), rowspec, rowspec, _full((1, D)), _full((D, w_in.shape[1])), tabspec, tabspec],
        out_specs=out_specs,
        out_shape=out_shape,
        compiler_params=_cp(("parallel",)),
        name="front_odd" if with_q else "front_odd_ctx",
    )(x2d, sh, sc, g0, w_in, cos, sin)


def _diff_attn_kernel(q_ref, kc_ref, vc_ref, kl_ref, vl_ref, lam_ref, sg_ref, o_ref, *, lam_init):
    q = q_ref[...]
    tq = q.shape[0]
    lane = lax.broadcasted_iota(I32, q.shape, 1)
    zero = jnp.zeros_like(q)
    qq = jnp.concatenate([jnp.where(lane < DIFF_HD, q, zero), jnp.where(lane >= DIFF_HD, q, zero)], axis=0)
    s_c = _dot_nt(qq, kc_ref[...])
    s_l = _dot_nt(qq, kl_ref[...])
    m = jnp.maximum(jnp.max(s_c, axis=-1, keepdims=True), jnp.max(s_l, axis=-1, keepdims=True))
    p_c = jnp.exp2(s_c - m)
    p_l = jnp.exp2(s_l - m)
    r = 1.0 / (jnp.sum(p_c, axis=-1, keepdims=True) + jnp.sum(p_l, axis=-1, keepdims=True))
    lp = lam_ref[...]
    lam = (jnp.exp(jnp.sum(lp[0:1, :] * lp[1:2, :], axis=-1, keepdims=True))
           - jnp.exp(jnp.sum(lp[2:3, :] * lp[3:4, :], axis=-1, keepdims=True)) + lam_init)
    r1 = r[:tq]
    r2 = r[tq:] * lam
    a_c = (p_c[:tq] * r1 - p_c[tq:] * r2).astype(BF16)
    a_l = (p_l[:tq] * r1 - p_l[tq:] * r2).astype(BF16)
    o = _dot(a_c, vc_ref[...]) + _dot(a_l, vl_ref[...])
    o_ref[...] = (_rms(o, sg_ref[...]) * (1.0 - lam_init)).astype(BF16)


def _diff_attn(q, k_ctx, v_ctx, k_lat, v_lat, lam_p, subln_g, nb, seq, ctx_len, lam_init):
    tq = min(TQ_DIFF, seq)
    nq = seq // tq
    return pl.pallas_call(
        functools.partial(_diff_attn_kernel, lam_init=lam_init),
        grid=(nb, DIFF_H, nq),
        in_specs=[pl.BlockSpec((tq, LANE), lambda b, h, i: (b * nq + i, h)),
                  pl.BlockSpec((ctx_len, LANE), lambda b, h, i: (b, h)),
                  pl.BlockSpec((ctx_len, LANE), lambda b, h, i: (b, h)),
                  pl.BlockSpec((seq, LANE), lambda b, h, i: (b, h)),
                  pl.BlockSpec((seq, LANE), lambda b, h, i: (b, h)),
                  _full((4, DIFF_HD)), _full((1, 2 * DIFF_HD))],
        out_specs=pl.BlockSpec((tq, LANE), lambda b, h, i: (b * nq + i, h)),
        out_shape=jax.ShapeDtypeStruct((nb * seq, DIFF_W), BF16),
        compiler_params=_cp(("parallel", "parallel", "arbitrary")),
        name="diff_attn",
    )(q, k_ctx, v_ctx, k_lat, v_lat, lam_p, subln_g)


def _fourier_kernel(ct_ref, st_ref, f_ref, cc_ref, sc_ref, o_ref):
    f = f_ref[...]
    p = _dot(ct_ref[...], f).astype(BF16)
    q = _dot(st_ref[...], f).astype(BF16)
    o_ref[...] = (_dot(p, cc_ref[...]) - _dot(q, sc_ref[...])).astype(BF16)


def _fourier(f2d, ct, st, cc, sc, nb, seq):
    tk = min(TK_FFT, seq)
    nk = seq // tk
    return pl.pallas_call(
        _fourier_kernel,
        grid=(nk, nb),
        in_specs=[pl.BlockSpec((tk, seq), lambda j, b: (j, 0)), pl.BlockSpec((tk, seq), lambda j, b: (j, 0)),
                  pl.BlockSpec((seq, FNET_W), lambda j, b: (b, 0)), _full((FNET_W, FNET_W)), _full((FNET_W, FNET_W))],
        out_specs=pl.BlockSpec((tk, FNET_W), lambda j, b: (b * nk + j, 0)),
        out_shape=jax.ShapeDtypeStruct((nb * seq, FNET_W), BF16),
        compiler_params=_cp(("arbitrary", "arbitrary")),
        name="fourier",
    )(ct, st, f2d, cc, sc)


def _moe_scatter_kernel(tv_ref, pos_ref, h_ref, xs_ref, zbuf, sem, zsem):
    ts = h_ref.shape[0]
    tm = zbuf.shape[0]

    @pl.when(pl.program_id(0) == 0)
    def _():
        zbuf[...] = jnp.zeros_like(zbuf)

        def zcopy(t):
            return pltpu.make_async_copy(zbuf, xs_ref.at[pl.ds(t * tm, tm)], zsem)

        def zissue(t, c):
            @pl.when(tv_ref[t] < tm)
            def _():
                zcopy(t).start()
            return c

        def zdrain(t, c):
            @pl.when(tv_ref[t] < tm)
            def _():
                zcopy(t).wait()
            return c

        lax.fori_loop(0, tv_ref.shape[0], zissue, 0)
        lax.fori_loop(0, tv_ref.shape[0], zdrain, 0)

    def copy(r, slot):
        return pltpu.make_async_copy(h_ref.at[pl.ds(r, 1)], xs_ref.at[pl.ds(pos_ref[slot, r], 1)], sem)

    def issue(r, c):
        copy(r, 0).start()
        copy(r, 1).start()
        return c

    def drain(r, c):
        copy(r, 0).wait()
        copy(r, 1).wait()
        return c

    lax.fori_loop(0, ts, issue, 0)
    lax.fori_loop(0, ts, drain, 0)


def _moe_scatter(tile_valid, pos, h2p, rows, tm):
    n = h2p.shape[0]
    ts = min(TS_MOE, n)
    grid_spec = pltpu.PrefetchScalarGridSpec(
        num_scalar_prefetch=1,
        grid=(n // ts,),
        in_specs=[pl.BlockSpec((TOP_K, ts), lambda i, tv: (0, i), memory_space=pltpu.SMEM),
                  pl.BlockSpec((ts, D // 2), lambda i, tv: (i, 0))],
        out_specs=pl.BlockSpec(memory_space=pl.ANY),
        scratch_shapes=[pltpu.VMEM((tm, D // 2), U32), pltpu.SemaphoreType.DMA, pltpu.SemaphoreType.DMA],
    )
    return pl.pallas_call(
        _moe_scatter_kernel,
        grid_spec=grid_spec,
        out_shape=jax.ShapeDtypeStruct((rows, D // 2), U32),
        compiler_params=_cp(("arbitrary",)),
        name="moe_scatter",
    )(tile_valid, pos, h2p)


def _moe_expert_kernel(te_ref, tv_ref, xs_ref, wg_ref, wu_ref, wd_ref, ys_ref):
    i = pl.program_id(0)
    nvalid = tv_ref[i]

    @pl.when(nvalid > 0)
    def _():
        x = _unpack_bf16_pairs(xs_ref[...]).astype(BF16)
        ys_ref[...] = _pack_bf16_pairs(_swiglu(x, wg_ref[0], wu_ref[0], wd_ref[0]))

    @pl.when(nvalid <= 0)
    def _():
        ys_ref[...] = jnp.zeros_like(ys_ref)


def _moe_experts(tile_expert, tile_valid, xs, wg, wu, wd):
    rows = xs.shape[0]
    tm = TM_EXP
    grid_spec = pltpu.PrefetchScalarGridSpec(
        num_scalar_prefetch=2,
        grid=(rows // tm,),
        in_specs=[pl.BlockSpec((tm, D // 2), lambda i, te, tv: (i, 0)),
                  pl.BlockSpec((1, D, D_FF), lambda i, te, tv: (te[i], 0, 0)),
                  pl.BlockSpec((1, D, D_FF), lambda i, te, tv: (te[i], 0, 0)),
                  pl.BlockSpec((1, D_FF, D), lambda i, te, tv: (te[i], 0, 0))],
        out_specs=pl.BlockSpec((tm, D // 2), lambda i, te, tv: (i, 0)),
    )
    return pl.pallas_call(
        _moe_expert_kernel,
        grid_spec=grid_spec,
        out_shape=jax.ShapeDtypeStruct((rows, D // 2), U32),
        compiler_params=_cp(("arbitrary",)),
        name="moe_experts",
    )(tile_expert, tile_valid, xs, wg, wu, wd)


def _moe_combine_kernel(pos_ref, ys_ref, rwt_ref, x_ref, g2_ref, n3_ref, o_ref, buf, sem):
    ts = x_ref.shape[0]

    def copy(r, slot):
        return pltpu.make_async_copy(ys_ref.at[pl.ds(pos_ref[slot, r], 1)], buf.at[slot, pl.ds(r, 1)], sem)

    def issue(r, c):
        copy(r, 0).start()
        copy(r, 1).start()
        return c

    def drain(r, c):
        copy(r, 0).wait()
        copy(r, 1).wait()
        return c

    lax.fori_loop(0, ts, issue, 0)
    lax.fori_loop(0, ts, drain, 0)
    w = rwt_ref[...]
    f = _unpack_bf16_pairs(buf[0]) * w[:, 0:1] + _unpack_bf16_pairs(buf[1]) * w[:, 1:2]
    o_ref[...] = x_ref[...] + g2_ref[0] * _rms(f, n3_ref[...])


def _moe_combine(pos, ys, rwt, x1, g2, n3, row_of, seq):
    n = x1.shape[0]
    ts = min(TS_MOE, seq)
    nper = seq // ts
    rowspec = pl.BlockSpec((1, 1, D), lambda i: (row_of(i // nper), 0, 0))
    return pl.pallas_call(
        _moe_combine_kernel,
        grid=(n // ts,),
        in_specs=[pl.BlockSpec((TOP_K, ts), lambda i: (0, i), memory_space=pltpu.SMEM),
                  pl.BlockSpec(memory_space=pl.ANY),
                  pl.BlockSpec((ts, LANE), lambda i: (i, 0)), pl.BlockSpec((ts, D), lambda i: (i, 0)),
                  rowspec, _full((1, D))],
        out_specs=pl.BlockSpec((ts, D), lambda i: (i, 0)),
        out_shape=jax.ShapeDtypeStruct((n, D), F32),
        scratch_shapes=[pltpu.VMEM((TOP_K, ts, D // 2), U32), pltpu.SemaphoreType.DMA],
        compiler_params=_cp(("arbitrary",)),
        name="moe_combine",
    )(pos, ys, rwt, x1, g2, n3)


def _route(ridx, n, tm):
    e_flat = jnp.concatenate([ridx[:, 0], ridx[:, 1]])
    onehot = (e_flat[:, None] == jnp.arange(N_EXP, dtype=I32)[None, :]).astype(I32)
    csum = jnp.cumsum(onehot, axis=0)
    rank = jnp.sum(csum * onehot, axis=1) - 1
    counts = csum[-1]
    ptiles = (counts + tm - 1) // tm
    tile_end = jnp.cumsum(ptiles)
    row_off = (tile_end - ptiles) * tm
    pos = (jnp.sum(onehot * row_off[None, :], axis=1) + rank).reshape(TOP_K, n)
    n_tiles = (TOP_K * n) // tm + N_EXP
    t = jnp.arange(n_tiles, dtype=I32)
    te = jnp.sum((t[:, None] >= tile_end[None, :]).astype(I32), axis=1)
    used = te < N_EXP
    last_e = jnp.max(jnp.where(counts > 0, jnp.arange(N_EXP, dtype=I32), 0))
    te_c = jnp.where(used, te, last_e).astype(I32)
    start = jnp.sum(jnp.where(t[:, None] >= tile_end[None, :], ptiles[None, :], 0), axis=1)
    left = counts[jnp.minimum(te, N_EXP - 1)] - (t - start) * tm
    tv = jnp.where(used, jnp.clip(left, 0, tm), 0).astype(I32)
    return pos.astype(I32), te_c, tv, n_tiles * tm


def _rope_tables(seq):
    rows = seq // GRID_W
    row = np.repeat(np.arange(rows, dtype=np.float64), GRID_W)
    col = np.tile(np.arange(GRID_W, dtype=np.float64), rows)
    n_freq = MLA_ROPE // 4
    inv = ROPE_BASE ** (-np.arange(n_freq, dtype=np.float64) / n_freq)
    ang = np.concatenate([row[:, None] * inv, col[:, None] * inv], axis=-1)
    cos, sin = np.cos(ang), np.sin(ang)
    cos128 = np.tile(cos, (1, 4)).astype(np.float32)
    sin128 = np.tile(sin, (1, 4)).astype(np.float32)
    sin_signed = np.tile(np.concatenate([-sin, sin], axis=-1), (1, 2)).astype(np.float32)
    return jnp.asarray(cos128), jnp.asarray(sin128), jnp.asarray(sin_signed)


def _dft_tables(seq):
    k = np.arange(seq, dtype=np.int64)
    ang = 2.0 * np.pi * ((k[:, None] * k[None, :]) % seq).astype(np.float64) / seq
    ct = (np.cos(ang) / np.sqrt(seq)).astype(np.float32)
    st = (np.sin(ang) / np.sqrt(seq)).astype(np.float32)
    c = np.arange(FNET_GD, dtype=np.int64)
    angc = 2.0 * np.pi * ((c[:, None] * c[None, :]) % FNET_GD).astype(np.float64) / FNET_GD
    eye = np.eye(FNET_G)
    cc = np.kron(eye, np.cos(angc) / np.sqrt(FNET_GD)).astype(np.float32)
    sc = np.kron(eye, np.sin(angc) / np.sqrt(FNET_GD)).astype(np.float32)
    return (jnp.asarray(ct, dtype=BF16), jnp.asarray(st, dtype=BF16), jnp.asarray(cc, dtype=BF16),
            jnp.asarray(sc, dtype=BF16))


def _rot_half_cols(w):
    return jnp.concatenate([-w[..., MLA_ROPE // 2:], w[..., :MLA_ROPE // 2]], axis=-1)


def _even_weights(w_in, w_uq, w_ukv):
    o = 3 * SC_W + MLA_QR + MLA_KVR
    kpe = w_in[:, o:o + MLA_ROPE]
    z64 = jnp.zeros((D, LANE - MLA_ROPE), w_in.dtype)
    w_in_ext = jnp.concatenate([w_in[:, :o], kpe, z64, _rot_half_cols(kpe), z64], axis=1).astype(BF16)
    wq = w_uq.reshape(MLA_QR, MLA_H, MLA_NOPE + MLA_ROPE)
    zq = jnp.zeros((MLA_QR, MLA_H, LANE - MLA_ROPE), w_uq.dtype)
    wq_main = jnp.concatenate([wq, zq], axis=-1).reshape(MLA_QR, MLA_H * 256).astype(BF16)
    wq_swap = jnp.concatenate([_rot_half_cols(wq[..., MLA_NOPE:]), zq], axis=-1).reshape(MLA_QR, MLA_H * LANE)
    return w_in_ext, wq_main, wq_swap.astype(BF16), w_ukv.astype(BF16)


def kernel(x, c, ctx, c_ctx, ev_mod_w, ev_mod_b, ev_norm_g, ev_w_in, ev_conv_w, ev_q_norm_g, ev_w_uq, ev_kv_norm_g,
           ev_w_ukv, ev_w_out, ev_ffn_gate, ev_ffn_up, ev_ffn_down, od_mod_w, od_mod_b, od_norm_g, od_w_in,
           od_lambda, od_subln_g, od_w_out, od_router, od_exp_gate, od_exp_up, od_exp_down):
    nb, seq, _ = x.shape
    ctx_len = ctx.shape[1]
    n = nb * seq
    nc = nb * ctx_len
    assert seq % GRID_W == 0 and seq % 128 == 0 and ctx_len % 128 == 0
    mod_rows = ((nb + 1 + 7) // 8) * 8
    cond = jnp.zeros((mod_rows, D), F32).at[:nb].set(c).at[nb].set(c_ctx)
    lat_row = lambda b: b
    ctx_row = lambda b: nb

    cos128, sin128, sin_signed = _rope_tables(seq)
    ones_c = jnp.ones((ctx_len, LANE), F32)
    zeros_c = jnp.zeros((ctx_len, LANE), F32)
    x2d = x.reshape(n, D)
    c2d = ctx.reshape(nc, D)

    mods = _modulation(cond, ev_mod_w[0].astype(BF16), ev_mod_b[0])
    sh1, sc1, g1, sh2, sc2, g2 = [m.reshape(mod_rows, 1, D) for m in jnp.split(mods, N_MOD, axis=-1)]
    ng = ev_norm_g[0].reshape(4, 1, D)
    w_in_e, wq_main, wq_swap, wkv = _even_weights(ev_w_in[0], ev_w_uq[0], ev_w_ukv[0])
    qg = ev_q_norm_g[0].reshape(1, MLA_QR)
    kvg = ev_kv_norm_g[0].reshape(1, MLA_KVR)
    w_out_e = ev_w_out[0].astype(BF16)
    wg_e, wu_e, wd_e = ev_ffn_gate[0].astype(BF16), ev_ffn_up[0].astype(BF16), ev_ffn_down[0].astype(BF16)

    bg_l, uc_l, q_l, k_l, v_l = _front_even(x2d, sh1, sc1, lat_row, ng[0], w_in_e, qg, wq_main, wq_swap, kvg, wkv,
                                            cos128, sin128, seq)
    bg_c, uc_c, q_c, k_c, v_c = _front_even(c2d, sh1, sc1, ctx_row, ng[0], w_in_e, qg, wq_main, wq_swap, kvg, wkv,
                                            ones_c, zeros_c, ctx_len)
    at_l = _mla_attn(q_l, k_c, v_c, k_l, v_l, nb, seq, ctx_len)
    at_c = _mla_attn(q_c, k_c, v_c, None, None, nb, ctx_len, ctx_len)
    x1_l, h2_l = _mix_even(bg_l, uc_l, ev_conv_w[0], at_l, w_out_e, x2d, g1, ng[1], ng[2], sh2, sc2, lat_row, seq)
    x1_c, h2_c = _mix_even(bg_c, uc_c, ev_conv_w[0], at_c, w_out_e, c2d, g1, ng[1], ng[2], sh2, sc2, ctx_row,
                           ctx_len)
    x2d = _ffn_dense(h2_l, wg_e, wu_e, wd_e, x1_l, g2, ng[3], lat_row, seq)
    c2d = _ffn_dense(h2_c, wg_e, wu_e, wd_e, x1_c, g2, ng[3], ctx_row, ctx_len)

    lam_init = 0.8 - 0.6 * math.exp(-0.3 * 1)
    mods = _modulation(cond, od_mod_w[0].astype(BF16), od_mod_b[0])
    sh1, sc1, g1, sh2, sc2, g2 = [m.reshape(mod_rows, 1, D) for m in jnp.split(mods, N_MOD, axis=-1)]
    ng = od_norm_g[0].reshape(4, 1, D)
    w_in_o = od_w_in[0].astype(BF16)
    q_o, k_o, v_o, f_o = _front_odd(x2d, sh1, sc1, lat_row, ng[0], w_in_o, cos128, sin_signed, seq, True)
    kc_o, vc_o = _front_odd(c2d, sh1, sc1, ctx_row, ng[0], w_in_o[:, DIFF_W:3 * DIFF_W], ones_c, zeros_c, ctx_len,
                            False)
    ca = _diff_attn(q_o, kc_o, vc_o, k_o, v_o, od_lambda[0], od_subln_g[0].reshape(1, 2 * DIFF_HD), nb, seq,
                    ctx_len, lam_init)
    ct, st, cc, sc = _dft_tables(seq)
    fd = _fourier(f_o, ct, st, cc, sc, nb, seq)
    router_pad = jnp.zeros((D, LANE), F32).at[:, :N_EXP].set(od_router[0])
    x1, h2p, ridx, rwt = _mix_odd(ca, fd, od_w_out[0].astype(BF16), x2d, g1, ng[1], ng[2], sh2, sc2, router_pad,
                                  lat_row, seq)
    pos, tile_expert, tile_valid, rows = _route(ridx, n, TM_EXP)
    xs = _moe_scatter(tile_valid, pos, h2p, rows, TM_EXP)
    ys = _moe_experts(tile_expert, tile_valid, xs, od_exp_gate[0].astype(BF16), od_exp_up[0].astype(BF16),
                      od_exp_down[0].astype(BF16))
    out = _moe_combine(pos, ys, rwt, x1, g2, ng[3], lat_row, seq)
    return out.reshape(nb, seq, D)
```

```python
import functools
import math

import numpy as np
import jax
import jax.numpy as jnp
from jax import lax
from jax.experimental import pallas as pl
from jax.experimental.pallas import tpu as pltpu

F32 = jnp.float32
BF16 = jnp.bfloat16
I32 = jnp.int32
U32 = jnp.uint32

D = 1024
GRID_W = 64
EPS = 1e-6
ROPE_BASE = 10000.0
N_MOD = 6
SC_W = D // 2
MLA_V = 128
MLA_NOPE = 128
MLA_ROPE = 64
MLA_H = (D - SC_W) // MLA_V
MLA_QR = 3 * D // 8
MLA_KVR = D // 4
MLA_SCALE = (MLA_NOPE + MLA_ROPE) ** -0.5
DIFF_W = 3 * D // 4
DIFF_HD = 64
DIFF_H = DIFF_W // (2 * DIFF_HD)
DIFF_SCALE = DIFF_HD ** -0.5
FNET_W = D - DIFF_W
FNET_G = 4
FNET_GD = FNET_W // FNET_G
D_FF = ((8 * D // 3 + 127) // 128) * 128
N_EXP = 8
TOP_K = 2
LOG2E = math.log2(math.e)

LANE = 128
VMEM_LIMIT = 56 * 1024 * 1024

TM = 512
TQ_MLA = 512
TQ_DIFF = 256
BK_ATTN = 512
HP_DIFF = 3
TK_FFT = 512
TM_EXP = 512
TS_MOE = 256


def _cp(sem, vmem=VMEM_LIMIT):
    return pltpu.CompilerParams(dimension_semantics=sem, vmem_limit_bytes=vmem)


def _rms(x, g):
    return x * lax.rsqrt(jnp.mean(x * x, axis=-1, keepdims=True) + EPS) * g


def _dot(a, b):
    return jnp.dot(a, b, preferred_element_type=F32)


def _dot_nt(a, b):
    return lax.dot_general(a, b, (((1,), (1,)), ((), ())), preferred_element_type=F32)


def _full(shape):
    nd = len(shape)
    return pl.BlockSpec(shape, lambda *_: (0,) * nd)


def _pack_bf16_pairs(x):
    n = x.shape[1] // 2
    hi = pltpu.bitcast(x[:, :n].astype(BF16).astype(F32), U32)
    lo = pltpu.bitcast(x[:, n:].astype(BF16).astype(F32), U32)
    return hi | (lo >> 16)


def _unpack_bf16_pairs(u):
    hi = pltpu.bitcast(u & jnp.uint32(0xFFFF0000), F32)
    lo = pltpu.bitcast(u << 16, F32)
    return jnp.concatenate([hi, lo], axis=1)


def _mod_kernel(c_ref, w_ref, b_ref, o_ref):
    c = c_ref[...]
    s = c / (1.0 + jnp.exp(-c))
    o_ref[...] = _dot(s.astype(BF16), w_ref[...]) + b_ref[...]


def _modulation(cond, w_bf, b):
    rows = cond.shape[0]
    n = w_bf.shape[1]
    tn = 1536
    return pl.pallas_call(
        _mod_kernel,
        grid=(n // tn,),
        in_specs=[_full((rows, D)), pl.BlockSpec((D, tn), lambda j: (0, j)), pl.BlockSpec((1, tn), lambda j: (0, j))],
        out_specs=pl.BlockSpec((rows, tn), lambda j: (0, j)),
        out_shape=jax.ShapeDtypeStruct((rows, n), F32),
        compiler_params=_cp(("arbitrary",)),
        name="modulation",
    )(cond, w_bf, b.reshape(1, n))


def _front_even_kernel(x_ref, sh_ref, sc_ref, g0_ref, win_ref, qg_ref, wq_ref, wqs_ref, kvg_ref, wkv_ref,
                       cos_ref, sin_ref, bg_ref, uc_ref, q_ref, k_ref, v_ref):
    x = x_ref[...]
    h = _rms(x, g0_ref[...]) * (1.0 + sc_ref[0]) + sh_ref[0]
    z = _dot(h.astype(BF16), win_ref[...])
    bg_ref[...] = z[:, 0:SC_W].astype(BF16)
    uc_ref[...] = (z[:, SC_W:2 * SC_W] * z[:, 2 * SC_W:3 * SC_W]).astype(BF16)
    o = 3 * SC_W
    zq = z[:, o:o + MLA_QR]
    zkv = z[:, o + MLA_QR:o + MLA_QR + MLA_KVR]
    o2 = o + MLA_QR + MLA_KVR
    kpe = z[:, o2:o2 + LANE]
    kpes = z[:, o2 + LANE:o2 + 2 * LANE]
    cos = cos_ref[...]
    sin = sin_ref[...]
    zqn = _rms(zq, qg_ref[...]).astype(BF16)
    qm = _dot(zqn, wq_ref[...])
    qs = _dot(zqn, wqs_ref[...])
    qscale = MLA_SCALE * LOG2E
    for hd in range(MLA_H):
        lo = qm[:, 256 * hd:256 * hd + LANE]
        hi = qm[:, 256 * hd + LANE:256 * hd + 2 * LANE] * cos + qs[:, LANE * hd:LANE * hd + LANE] * sin
        q_ref[hd, :, 0:LANE] = (lo * qscale).astype(BF16)
        q_ref[hd, :, LANE:2 * LANE] = (hi * qscale).astype(BF16)
    zkvn = _rms(zkv, kvg_ref[...]).astype(BF16)
    kv = _dot(zkvn, wkv_ref[...])
    kpr = (kpe * cos + kpes * sin).astype(BF16)
    for hd in range(MLA_H):
        k_ref[hd, :, 0:LANE] = kv[:, 256 * hd:256 * hd + LANE].astype(BF16)
        k_ref[hd, :, LANE:2 * LANE] = kpr
        v_ref[hd] = kv[:, 256 * hd + LANE:256 * hd + 2 * LANE].astype(BF16)


def _front_even(x2d, sh, sc, row_of, g0, w_in, qg, wq, wqs, kvg, wkv, cos, sin, seq):
    n = x2d.shape[0]
    tm = min(TM, seq)
    nper = seq // tm
    rowspec = pl.BlockSpec((1, 1, D), lambda i: (row_of(i // nper), 0, 0))
    tabspec = pl.BlockSpec((tm, LANE), lambda i: (i % nper, 0))
    win_n = w_in.shape[1]
    return pl.pallas_call(
        _front_even_kernel,
        grid=(n // tm,),
        in_specs=[pl.BlockSpec((tm, D), lambda i: (i, 0)), rowspec, rowspec, _full((1, D)), _full((D, win_n)),
                  _full((1, MLA_QR)), _full((MLA_QR, 4 * 256)), _full((MLA_QR, 4 * LANE)),
                  _full((1, MLA_KVR)), _full((MLA_KVR, 4 * 256)), tabspec, tabspec],
        out_specs=[pl.BlockSpec((tm, SC_W), lambda i: (i, 0)), pl.BlockSpec((tm, SC_W), lambda i: (i, 0)),
                   pl.BlockSpec((MLA_H, tm, 256), lambda i: (0, i, 0)),
                   pl.BlockSpec((MLA_H, tm, 256), lambda i: (0, i, 0)),
                   pl.BlockSpec((MLA_H, tm, LANE), lambda i: (0, i, 0))],
        out_shape=[jax.ShapeDtypeStruct((n, SC_W), BF16), jax.ShapeDtypeStruct((n, SC_W), BF16),
                   jax.ShapeDtypeStruct((MLA_H, n, 256), BF16), jax.ShapeDtypeStruct((MLA_H, n, 256), BF16),
                   jax.ShapeDtypeStruct((MLA_H, n, LANE), BF16)],
        compiler_params=_cp(("parallel",)),
        name="front_even",
    )(x2d, sh, sc, g0, w_in, qg, wq, wqs, kvg, wkv, cos, sin)


def _key_blocks(k_ref, v_ref, bk):
    n = k_ref.shape[0]
    return [(k_ref.at[pl.ds(j, min(bk, n - j))], v_ref.at[pl.ds(j, min(bk, n - j))]) for j in range(0, n, bk)]


def _online_softmax_pv(q, blocks):
    m = acc = None
    dv = blocks[0][1].shape[1]
    for k_blk, v_blk in blocks:
        bk = k_blk.shape[0]
        ones_col = (lax.broadcasted_iota(I32, (bk, LANE), 1) == 0).astype(BF16)
        v_ext = jnp.concatenate([v_blk[...], ones_col], axis=1)
        s = _dot_nt(q, k_blk[...])
        mb = jnp.max(s, axis=-1, keepdims=True)
        if m is None:
            m = mb
            acc = _dot(jnp.exp2(s - m).astype(BF16), v_ext)
        else:
            m_new = jnp.maximum(m, mb)
            acc = jnp.exp2(m - m_new) * acc + _dot(jnp.exp2(s - m_new).astype(BF16), v_ext)
            m = m_new
    return acc[:, :dv], acc[:, dv:dv + 1]


def _mla_attn_kernel(*refs, with_lat, bk):
    if with_lat:
        q_ref, kc_ref, vc_ref, kl_ref, vl_ref, o_ref = refs
    else:
        q_ref, kc_ref, vc_ref, o_ref = refs
    blocks = _key_blocks(kc_ref.at[0], vc_ref.at[0], bk)
    if with_lat:
        blocks += _key_blocks(kl_ref.at[0], vl_ref.at[0], bk)
    acc, l = _online_softmax_pv(q_ref[0], blocks)
    o_ref[...] = (acc * (1.0 / l)).astype(BF16)


def _mla_attn(q, k_ctx, v_ctx, k_lat, v_lat, nb, seq_q, ctx_len):
    with_lat = k_lat is not None
    tq = min(TQ_MLA, seq_q)
    nq = seq_q // tq
    in_specs = [pl.BlockSpec((1, tq, 256), lambda b, h, i: (h, b * nq + i, 0)),
                pl.BlockSpec((1, ctx_len, 256), lambda b, h, i: (h, b, 0)),
                pl.BlockSpec((1, ctx_len, LANE), lambda b, h, i: (h, b, 0))]
    args = [q, k_ctx, v_ctx]
    if with_lat:
        in_specs += [pl.BlockSpec((1, seq_q, 256), lambda b, h, i: (h, b, 0)),
                     pl.BlockSpec((1, seq_q, LANE), lambda b, h, i: (h, b, 0))]
        args += [k_lat, v_lat]
    return pl.pallas_call(
        functools.partial(_mla_attn_kernel, with_lat=with_lat, bk=BK_ATTN),
        grid=(nb, MLA_H, nq),
        in_specs=in_specs,
        out_specs=pl.BlockSpec((tq, LANE), lambda b, h, i: (b * nq + i, h)),
        out_shape=jax.ShapeDtypeStruct((nb * seq_q, MLA_H * MLA_V), BF16),
        compiler_params=_cp(("parallel", "parallel", "arbitrary")),
        name="mla_attn_lat" if with_lat else "mla_attn_ctx",
    )(*args)


def _residual_and_h2(y, x_ref, g1_ref, n1_ref, n2_ref, sh2_ref, sc2_ref):
    x1 = x_ref[...] + g1_ref[0] * _rms(y, n1_ref[...])
    h2 = _rms(x1, n2_ref[...]) * (1.0 + sc2_ref[0]) + sh2_ref[0]
    return x1, h2


def _mix_even_kernel(bg_ref, uc_ref, ucp_ref, ucn_ref, cw_ref, at_ref, wo_ref, x_ref, g1_ref, n1_ref, n2_ref,
                     sh2_ref, sc2_ref, x1_ref, h2_ref, scr, *, nper):
    tm = uc_ref.shape[0]
    i = pl.program_id(0)
    ucf = uc_ref[...].astype(F32)
    first = (i % nper) == 0
    last = (i % nper) == nper - 1
    prev_row = jnp.where(first, 0.0, ucp_ref[7:8, :].astype(F32))
    next_row = jnp.where(last, 0.0, ucn_ref[0:1, :].astype(F32))
    scr[8:8 + tm, :] = ucf
    scr[7:8, :] = prev_row
    scr[8 + tm:9 + tm, :] = next_row
    up = scr[7:7 + tm, :]
    dn = scr[9:9 + tm, :]
    conv = cw_ref[0:1, :] * up + cw_ref[1:2, :] * ucf + cw_ref[2:3, :] * dn
    a = (bg_ref[...].astype(F32) * conv).astype(BF16)
    y = _dot(a, wo_ref[0:SC_W, :]) + _dot(at_ref[...], wo_ref[SC_W:D, :])
    x1, h2 = _residual_and_h2(y, x_ref, g1_ref, n1_ref, n2_ref, sh2_ref, sc2_ref)
    x1_ref[...] = x1
    h2_ref[...] = h2.astype(BF16)


def _mix_even(bg, uc, conv_w, attn, w_out, x2d, g1, n1, n2, sh2, sc2, row_of, seq):
    n = x2d.shape[0]
    tm = min(TM, seq)
    nper = seq // tm
    nb8 = n // 8
    rowspec = pl.BlockSpec((1, 1, D), lambda i: (row_of(i // nper), 0, 0))
    tile = lambda w: pl.BlockSpec((tm, w), lambda i: (i, 0))
    return pl.pallas_call(
        functools.partial(_mix_even_kernel, nper=nper),
        grid=(n // tm,),
        in_specs=[tile(SC_W), tile(SC_W),
                  pl.BlockSpec((8, SC_W), lambda i: (jnp.maximum(i * (tm // 8) - 1, 0), 0)),
                  pl.BlockSpec((8, SC_W), lambda i: (jnp.minimum((i + 1) * (tm // 8), nb8 - 1), 0)),
                  _full((3, SC_W)), tile(MLA_H * MLA_V), _full((D, D)), tile(D), rowspec, _full((1, D)),
                  _full((1, D)), rowspec, rowspec],
        out_specs=[tile(D), tile(D)],
        out_shape=[jax.ShapeDtypeStruct((n, D), F32), jax.ShapeDtypeStruct((n, D), BF16)],
        scratch_shapes=[pltpu.VMEM((tm + 16, SC_W), F32)],
        compiler_params=_cp(("parallel",)),
        name="mix_even",
    )(bg, uc, uc, uc, conv_w, attn, w_out, x2d, g1, n1, n2, sh2, sc2)


def _mix_odd_kernel(ca_ref, fd_ref, wo_ref, x_ref, g1_ref, n1_ref, n2_ref, sh2_ref, sc2_ref, rw_ref,
                    x1_ref, h2p_ref, ridx_ref, rwt_ref):
    y = _dot(ca_ref[...], wo_ref[0:DIFF_W, :]) + _dot(fd_ref[...], wo_ref[DIFF_W:D, :])
    x1, h2 = _residual_and_h2(y, x_ref, g1_ref, n1_ref, n2_ref, sh2_ref, sc2_ref)
    x1_ref[...] = x1
    h2p_ref[...] = _pack_bf16_pairs(h2)
    tm = h2.shape[0]
    hi = h2.astype(BF16)
    lo = (h2 - hi.astype(F32)).astype(BF16)
    r = _dot(jnp.concatenate([hi, lo], axis=0), rw_ref[...])
    logits = (r[:tm, :LANE] + r[:tm, LANE:]) + (r[tm:, :LANE] + r[tm:, LANE:])
    lane = lax.broadcasted_iota(I32, logits.shape, 1).astype(F32)
    neg = jnp.float32(-jnp.inf)
    s0 = jnp.where(lane < N_EXP, logits, neg)
    m1 = jnp.max(s0, axis=-1, keepdims=True)
    i1 = jnp.min(jnp.where(s0 == m1, lane, float(LANE)), axis=-1, keepdims=True)
    s1 = jnp.where(lane == i1, neg, s0)
    m2 = jnp.max(s1, axis=-1, keepdims=True)
    i2 = jnp.min(jnp.where(s1 == m2, lane, float(LANE)), axis=-1, keepdims=True)
    e = jnp.exp(m2 - m1)
    w1 = 1.0 / (1.0 + e)
    w2 = e * w1
    ridx_ref[...] = jnp.where(lane == 0.0, i1, jnp.where(lane == 1.0, i2, 0.0)).astype(I32)
    rwt_ref[...] = jnp.where(lane == 0.0, w1, jnp.where(lane == 1.0, w2, 0.0))


def _mix_odd(cattn, fd, w_out, x2d, g1, n1, n2, sh2, sc2, router_pad, row_of, seq):
    n = x2d.shape[0]
    tm = min(TM, seq)
    nper = seq // tm
    rowspec = pl.BlockSpec((1, 1, D), lambda i: (row_of(i // nper), 0, 0))
    tile = lambda w: pl.BlockSpec((tm, w), lambda i: (i, 0))
    return pl.pallas_call(
        _mix_odd_kernel,
        grid=(n // tm,),
        in_specs=[tile(DIFF_W), tile(FNET_W), _full((D, D)), tile(D), rowspec, _full((1, D)), _full((1, D)),
                  rowspec, rowspec, _full((D, 2 * LANE))],
        out_specs=[tile(D), tile(D // 2), tile(LANE), tile(LANE)],
        out_shape=[jax.ShapeDtypeStruct((n, D), F32), jax.ShapeDtypeStruct((n, D // 2), U32),
                   jax.ShapeDtypeStruct((n, LANE), I32), jax.ShapeDtypeStruct((n, LANE), F32)],
        compiler_params=_cp(("parallel",)),
        name="mix_odd",
    )(cattn, fd, w_out, x2d, g1, n1, n2, sh2, sc2, router_pad)


def _swiglu(h, wg, wu, wd):
    g = _dot(h, wg)
    u = _dot(h, wu)
    a = (g / (1.0 + jnp.exp(-g)) * u).astype(BF16)
    return _dot(a, wd)


def _ffn_kernel(h_ref, wg_ref, wu_ref, wd_ref, x_ref, g2_ref, n3_ref, o_ref):
    f = _swiglu(h_ref[...], wg_ref[...], wu_ref[...], wd_ref[...])
    o_ref[...] = x_ref[...] + g2_ref[0] * _rms(f, n3_ref[...])


def _ffn_dense(h2, wg, wu, wd, x1, g2, n3, row_of, seq):
    n = x1.shape[0]
    tm = min(TM, seq)
    nper = seq // tm
    rowspec = pl.BlockSpec((1, 1, D), lambda i: (row_of(i // nper), 0, 0))
    tile = lambda w: pl.BlockSpec((tm, w), lambda i: (i, 0))
    once = lambda shape: pl.BlockSpec(shape, lambda i: (0, 0), pipeline_mode=pl.Buffered(1))
    return pl.pallas_call(
        _ffn_kernel,
        grid=(n // tm,),
        in_specs=[tile(D), once((D, D_FF)), once((D, D_FF)), once((D_FF, D)), tile(D), rowspec, _full((1, D))],
        out_specs=tile(D),
        out_shape=jax.ShapeDtypeStruct((n, D), F32),
        compiler_params=_cp(("parallel",)),
        name="ffn_dense",
    )(h2, wg, wu, wd, x1, g2, n3)


def _rope_slab(x, cos, sin_signed, lane):
    swap = jnp.where((lane & 63) < 32, pltpu.roll(x, 96, 1), pltpu.roll(x, 32, 1))
    return x * cos + swap * sin_signed


def _front_odd_kernel(x_ref, sh_ref, sc_ref, g0_ref, win_ref, cos_ref, sin_ref, q_ref, k_ref, v_ref, f_ref,
                      *, with_q):
    x = x_ref[...]
    h = _rms(x, g0_ref[...]) * (1.0 + sc_ref[0]) + sh_ref[0]
    z = _dot(h.astype(BF16), win_ref[...])
    cos = cos_ref[...]
    sin = sin_ref[...]
    lane = lax.broadcasted_iota(I32, cos.shape, 1)
    off = DIFF_W if with_q else 0
    qscale = DIFF_SCALE * LOG2E
    for g in range(DIFF_W // LANE):
        sl = slice(LANE * g, LANE * g + LANE)
        if with_q:
            q_ref[:, sl] = (_rope_slab(z[:, sl], cos, sin, lane) * qscale).astype(BF16)
        ksl = slice(off + LANE * g, off + LANE * g + LANE)
        k_ref[:, sl] = _rope_slab(z[:, ksl], cos, sin, lane).astype(BF16)
    v_ref[...] = z[:, off + DIFF_W:off + 2 * DIFF_W].astype(BF16)
    if with_q:
        f_ref[...] = z[:, 3 * DIFF_W:3 * DIFF_W + FNET_W].astype(BF16)


def _front_odd(x2d, sh, sc, row_of, g0, w_in, cos, sin, seq, with_q):
    n = x2d.shape[0]
    tm = min(TM, seq)
    nper = seq // tm
    rowspec = pl.BlockSpec((1, 1, D), lambda i: (row_of(i // nper), 0, 0))
    tabspec = pl.BlockSpec((tm, LANE), lambda i: (i % nper, 0))
    tile = lambda w: pl.BlockSpec((tm, w), lambda i: (i, 0))
    if with_q:
        kern = functools.partial(_front_odd_kernel, with_q=True)
        out_specs = [tile(DIFF_W), tile(DIFF_W), tile(DIFF_W), tile(FNET_W)]
        out_shape = [jax.ShapeDtypeStruct((n, DIFF_W), BF16)] * 3 + [jax.ShapeDtypeStruct((n, FNET_W), BF16)]
    else:
        def kern(x_ref, sh_ref, sc_ref, g0_ref, win_ref, cos_ref, sin_ref, k_ref, v_ref):
            _front_odd_kernel(x_ref, sh_ref, sc_ref, g0_ref, win_ref, cos_ref, sin_ref, None, k_ref, v_ref, None,
                              with_q=False)
        out_specs = [tile(DIFF_W), tile(DIFF_W)]
        out_shape = [jax.ShapeDtypeStruct((n, DIFF_W), BF16)] * 2
    return pl.pallas_call(
        kern,
        grid=(n // tm,),
        in_specs=[tile(D), rowspec, rowspec, _full((1, D)), _full((D, w_in.shape[1])), tabspec, tabspec],
        out_specs=out_specs,
        out_shape=out_shape,
        compiler_params=_cp(("parallel",)),
        name="front_odd" if with_q else "front_odd_ctx",
    )(x2d, sh, sc, g0, w_in, cos, sin)


def _diff_attn_kernel(q_ref, kc_ref, vc_ref, kl_ref, vl_ref, lam_ref, sg_ref, o_ref, *, lam_init, bk):
    tq = q_ref.shape[0]
    lp = lam_ref[...]
    lam = (jnp.exp(jnp.sum(lp[0:1, :] * lp[1:2, :], axis=-1, keepdims=True))
           - jnp.exp(jnp.sum(lp[2:3, :] * lp[3:4, :], axis=-1, keepdims=True)) + lam_init)
    lane = lax.broadcasted_iota(I32, (tq, LANE), 1)
    for hd in range(q_ref.shape[1] // LANE):
        sl = pl.ds(hd * LANE, LANE)
        q = q_ref[:, sl]
        zero = jnp.zeros_like(q)
        qq = jnp.concatenate([jnp.where(lane < DIFF_HD, q, zero), jnp.where(lane >= DIFF_HD, q, zero)], axis=0)
        blocks = (_key_blocks(kc_ref.at[:, sl], vc_ref.at[:, sl], bk)
                  + _key_blocks(kl_ref.at[:, sl], vl_ref.at[:, sl], bk))
        acc, l = _online_softmax_pv(qq, blocks)
        r = 1.0 / l
        o = acc[:tq] * r[:tq] - acc[tq:] * (r[tq:] * lam)
        o_ref[:, sl] = (_rms(o, sg_ref[...]) * (1.0 - lam_init)).astype(BF16)


def _diff_attn(q, k_ctx, v_ctx, k_lat, v_lat, lam_p, subln_g, nb, seq, ctx_len, lam_init):
    tq = min(TQ_DIFF, seq)
    nq = seq // tq
    return pl.pallas_call(
        functools.partial(_diff_attn_kernel, lam_init=lam_init, bk=BK_ATTN),
        grid=(nb, DIFF_H // HP_DIFF, nq),
        in_specs=[pl.BlockSpec((tq, HP_DIFF * LANE), lambda b, h, i: (b * nq + i, h)),
                  pl.BlockSpec((ctx_len, HP_DIFF * LANE), lambda b, h, i: (b, h)),
                  pl.BlockSpec((ctx_len, HP_DIFF * LANE), lambda b, h, i: (b, h)),
                  pl.BlockSpec((seq, HP_DIFF * LANE), lambda b, h, i: (b, h)),
                  pl.BlockSpec((seq, HP_DIFF * LANE), lambda b, h, i: (b, h)),
                  _full((4, DIFF_HD)), _full((1, 2 * DIFF_HD))],
        out_specs=pl.BlockSpec((tq, HP_DIFF * LANE), lambda b, h, i: (b * nq + i, h)),
        out_shape=jax.ShapeDtypeStruct((nb * seq, DIFF_W), BF16),
        compiler_params=_cp(("parallel", "parallel", "arbitrary")),
        name="diff_attn",
    )(q, k_ctx, v_ctx, k_lat, v_lat, lam_p, subln_g)


def _fourier_kernel(ct_ref, st_ref, f_ref, cc_ref, sc_ref, o_ref):
    f = f_ref[...]
    p = _dot(ct_ref[...], f).astype(BF16)
    q = _dot(st_ref[...], f).astype(BF16)
    o_ref[...] = (_dot(p, cc_ref[...]) - _dot(q, sc_ref[...])).astype(BF16)


def _fourier(f2d, ct, st, cc, sc, nb, seq):
    tk = min(TK_FFT, seq)
    nk = seq // tk
    return pl.pallas_call(
        _fourier_kernel,
        grid=(nk, nb),
        in_specs=[pl.BlockSpec((tk, seq), lambda j, b: (j, 0)), pl.BlockSpec((tk, seq), lambda j, b: (j, 0)),
                  pl.BlockSpec((seq, FNET_W), lambda j, b: (b, 0)), _full((FNET_W, FNET_W)), _full((FNET_W, FNET_W))],
        out_specs=pl.BlockSpec((tk, FNET_W), lambda j, b: (b * nk + j, 0)),
        out_shape=jax.ShapeDtypeStruct((nb * seq, FNET_W), BF16),
        compiler_params=_cp(("arbitrary", "arbitrary")),
        name="fourier",
    )(ct, st, f2d, cc, sc)


def _moe_scatter_kernel(tv_ref, pos_ref, h_ref, xs_ref, zbuf, sem, zsem):
    ts = h_ref.shape[0]
    tm = zbuf.shape[0]

    @pl.when(pl.program_id(0) == 0)
    def _():
        zbuf[...] = jnp.zeros_like(zbuf)

        def zcopy(t):
            return pltpu.make_async_copy(zbuf, xs_ref.at[pl.ds(t * tm, tm)], zsem)

        def zissue(t, c):
            @pl.when(tv_ref[t] < tm)
            def _():
                zcopy(t).start()
            return c

        def zdrain(t, c):
            @pl.when(tv_ref[t] < tm)
            def _():
                zcopy(t).wait()
            return c

        lax.fori_loop(0, tv_ref.shape[0], zissue, 0)
        lax.fori_loop(0, tv_ref.shape[0], zdrain, 0)

    def issue(r, c):
        for slot in range(TOP_K):
            pltpu.make_async_copy(h_ref.at[pl.ds(r, 1)], xs_ref.at[pl.ds(pos_ref[slot, r], 1)],
                                  sem.at[slot]).start(priority=slot)
        return c

    lax.fori_loop(0, ts, issue, 0, unroll=8)
    for slot in range(TOP_K):
        pltpu.make_async_copy(h_ref, xs_ref.at[pl.ds(0, ts)], sem.at[slot]).wait()


def _moe_scatter(tile_valid, pos, h2p, rows, tm):
    n = h2p.shape[0]
    ts = min(TS_MOE, n)
    grid_spec = pltpu.PrefetchScalarGridSpec(
        num_scalar_prefetch=1,
        grid=(n // ts,),
        in_specs=[pl.BlockSpec((TOP_K, ts), lambda i, tv: (0, i), memory_space=pltpu.SMEM),
                  pl.BlockSpec((ts, D // 2), lambda i, tv: (i, 0))],
        out_specs=pl.BlockSpec(memory_space=pl.ANY),
        scratch_shapes=[pltpu.VMEM((tm, D // 2), U32), pltpu.SemaphoreType.DMA((TOP_K,)), pltpu.SemaphoreType.DMA],
    )
    return pl.pallas_call(
        _moe_scatter_kernel,
        grid_spec=grid_spec,
        out_shape=jax.ShapeDtypeStruct((rows, D // 2), U32),
        compiler_params=_cp(("arbitrary",)),
        name="moe_scatter",
    )(tile_valid, pos, h2p)


def _moe_expert_kernel(te_ref, tv_ref, xs_ref, wg_ref, wu_ref, wd_ref, ys_ref):
    i = pl.program_id(0)
    nvalid = tv_ref[i]

    @pl.when(nvalid > 0)
    def _():
        x = _unpack_bf16_pairs(xs_ref[...]).astype(BF16)
        ys_ref[...] = _pack_bf16_pairs(_swiglu(x, wg_ref[0], wu_ref[0], wd_ref[0]))

    @pl.when(nvalid <= 0)
    def _():
        ys_ref[...] = jnp.zeros_like(ys_ref)


def _moe_experts(tile_expert, tile_valid, xs, wg, wu, wd):
    rows = xs.shape[0]
    tm = TM_EXP
    grid_spec = pltpu.PrefetchScalarGridSpec(
        num_scalar_prefetch=2,
        grid=(rows // tm,),
        in_specs=[pl.BlockSpec((tm, D // 2), lambda i, te, tv: (i, 0)),
                  pl.BlockSpec((1, D, D_FF), lambda i, te, tv: (te[i], 0, 0)),
                  pl.BlockSpec((1, D, D_FF), lambda i, te, tv: (te[i], 0, 0)),
                  pl.BlockSpec((1, D_FF, D), lambda i, te, tv: (te[i], 0, 0))],
        out_specs=pl.BlockSpec((tm, D // 2), lambda i, te, tv: (i, 0)),
    )
    return pl.pallas_call(
        _moe_expert_kernel,
        grid_spec=grid_spec,
        out_shape=jax.ShapeDtypeStruct((rows, D // 2), U32),
        compiler_params=_cp(("arbitrary",)),
        name="moe_experts",
    )(tile_expert, tile_valid, xs, wg, wu, wd)


def _moe_combine_kernel(pos_ref, posn_ref, ys_ref, rwt_ref, x_ref, g2_ref, n3_ref, o_ref, buf, sem):
    ts = x_ref.shape[0]
    i = pl.program_id(0)
    cur = i % 2

    def gather(p_ref, b):
        def issue(r, c):
            for slot in range(TOP_K):
                pltpu.make_async_copy(ys_ref.at[pl.ds(p_ref[slot, r], 1)], buf.at[b, slot, pl.ds(r, 1)],
                                      sem.at[b, slot]).start(priority=slot)
            return c
        lax.fori_loop(0, ts, issue, 0, unroll=8)

    @pl.when(i == 0)
    def _():
        gather(pos_ref, 0)

    @pl.when(i + 1 < pl.num_programs(0))
    def _():
        gather(posn_ref, 1 - cur)

    for slot in range(TOP_K):
        pltpu.make_async_copy(ys_ref.at[pl.ds(0, ts)], buf.at[cur, slot], sem.at[cur, slot]).wait()
    w = rwt_ref[...]
    f = _unpack_bf16_pairs(buf[cur, 0]) * w[:, 0:1] + _unpack_bf16_pairs(buf[cur, 1]) * w[:, 1:2]
    o_ref[...] = x_ref[...] + g2_ref[0] * _rms(f, n3_ref[...])


def _moe_combine(pos, ys, rwt, x1, g2, n3, row_of, seq):
    n = x1.shape[0]
    ts = min(TS_MOE, seq)
    nper = seq // ts
    rowspec = pl.BlockSpec((1, 1, D), lambda i: (row_of(i // nper), 0, 0))
    nsteps = n // ts
    return pl.pallas_call(
        _moe_combine_kernel,
        grid=(nsteps,),
        in_specs=[pl.BlockSpec((TOP_K, ts), lambda i: (0, i), memory_space=pltpu.SMEM),
                  pl.BlockSpec((TOP_K, ts), lambda i: (0, jnp.minimum(i + 1, nsteps - 1)), memory_space=pltpu.SMEM),
                  pl.BlockSpec(memory_space=pl.ANY),
                  pl.BlockSpec((ts, LANE), lambda i: (i, 0)), pl.BlockSpec((ts, D), lambda i: (i, 0)),
                  rowspec, _full((1, D))],
        out_specs=pl.BlockSpec((ts, D), lambda i: (i, 0)),
        out_shape=jax.ShapeDtypeStruct((n, D), F32),
        scratch_shapes=[pltpu.VMEM((2, TOP_K, ts, D // 2), U32), pltpu.SemaphoreType.DMA((2, TOP_K))],
        compiler_params=_cp(("arbitrary",)),
        name="moe_combine",
    )(pos, pos, ys, rwt, x1, g2, n3)


def _route(ridx, n, tm):
    e_flat = jnp.concatenate([ridx[:, 0], ridx[:, 1]])
    onehot = (e_flat[:, None] == jnp.arange(N_EXP, dtype=I32)[None, :]).astype(I32)
    csum = jnp.cumsum(onehot, axis=0)
    rank = jnp.sum(csum * onehot, axis=1) - 1
    counts = csum[-1]
    ptiles = (counts + tm - 1) // tm
    tile_end = jnp.cumsum(ptiles)
    row_off = (tile_end - ptiles) * tm
    pos = (jnp.sum(onehot * row_off[None, :], axis=1) + rank).reshape(TOP_K, n)
    n_tiles = (TOP_K * n) // tm + N_EXP
    t = jnp.arange(n_tiles, dtype=I32)
    te = jnp.sum((t[:, None] >= tile_end[None, :]).astype(I32), axis=1)
    used = te < N_EXP
    last_e = jnp.max(jnp.where(counts > 0, jnp.arange(N_EXP, dtype=I32), 0))
    te_c = jnp.where(used, te, last_e).astype(I32)
    start = jnp.sum(jnp.where(t[:, None] >= tile_end[None, :], ptiles[None, :], 0), axis=1)
    left = counts[jnp.minimum(te, N_EXP - 1)] - (t - start) * tm
    tv = jnp.where(used, jnp.clip(left, 0, tm), 0).astype(I32)
    return pos.astype(I32), te_c, tv, n_tiles * tm


def _rope_tables(seq):
    rows = seq // GRID_W
    row = np.repeat(np.arange(rows, dtype=np.float64), GRID_W)
    col = np.tile(np.arange(GRID_W, dtype=np.float64), rows)
    n_freq = MLA_ROPE // 4
    inv = ROPE_BASE ** (-np.arange(n_freq, dtype=np.float64) / n_freq)
    ang = np.concatenate([row[:, None] * inv, col[:, None] * inv], axis=-1)
    cos, sin = np.cos(ang), np.sin(ang)
    cos128 = np.tile(cos, (1, 4)).astype(np.float32)
    sin128 = np.tile(sin, (1, 4)).astype(np.float32)
    sin_signed = np.tile(np.concatenate([-sin, sin], axis=-1), (1, 2)).astype(np.float32)
    return jnp.asarray(cos128), jnp.asarray(sin128), jnp.asarray(sin_signed)


def _dft_tables(seq):
    k = np.arange(seq, dtype=np.int64)
    ang = 2.0 * np.pi * ((k[:, None] * k[None, :]) % seq).astype(np.float64) / seq
    ct = (np.cos(ang) / np.sqrt(seq)).astype(np.float32)
    st = (np.sin(ang) / np.sqrt(seq)).astype(np.float32)
    c = np.arange(FNET_GD, dtype=np.int64)
    angc = 2.0 * np.pi * ((c[:, None] * c[None, :]) % FNET_GD).astype(np.float64) / FNET_GD
    eye = np.eye(FNET_G)
    cc = np.kron(eye, np.cos(angc) / np.sqrt(FNET_GD)).astype(np.float32)
    sc = np.kron(eye, np.sin(angc) / np.sqrt(FNET_GD)).astype(np.float32)
    return (jnp.asarray(ct, dtype=BF16), jnp.asarray(st, dtype=BF16), jnp.asarray(cc, dtype=BF16),
            jnp.asarray(sc, dtype=BF16))


def _rot_half_cols(w):
    return jnp.concatenate([-w[..., MLA_ROPE // 2:], w[..., :MLA_ROPE // 2]], axis=-1)


def _even_weights(w_in, w_uq, w_ukv):
    o = 3 * SC_W + MLA_QR + MLA_KVR
    kpe = w_in[:, o:o + MLA_ROPE]
    z64 = jnp.zeros((D, LANE - MLA_ROPE), w_in.dtype)
    w_in_ext = jnp.concatenate([w_in[:, :o], kpe, z64, _rot_half_cols(kpe), z64], axis=1).astype(BF16)
    wq = w_uq.reshape(MLA_QR, MLA_H, MLA_NOPE + MLA_ROPE)
    zq = jnp.zeros((MLA_QR, MLA_H, LANE - MLA_ROPE), w_uq.dtype)
    wq_main = jnp.concatenate([wq, zq], axis=-1).reshape(MLA_QR, MLA_H * 256).astype(BF16)
    wq_swap = jnp.concatenate([_rot_half_cols(wq[..., MLA_NOPE:]), zq], axis=-1).reshape(MLA_QR, MLA_H * LANE)
    return w_in_ext, wq_main, wq_swap.astype(BF16), w_ukv.astype(BF16)


def kernel(x, c, ctx, c_ctx, ev_mod_w, ev_mod_b, ev_norm_g, ev_w_in, ev_conv_w, ev_q_norm_g, ev_w_uq, ev_kv_norm_g,
           ev_w_ukv, ev_w_out, ev_ffn_gate, ev_ffn_up, ev_ffn_down, od_mod_w, od_mod_b, od_norm_g, od_w_in,
           od_lambda, od_subln_g, od_w_out, od_router, od_exp_gate, od_exp_up, od_exp_down):
    nb, seq, _ = x.shape
    ctx_len = ctx.shape[1]
    n = nb * seq
    nc = nb * ctx_len
    assert seq % GRID_W == 0 and seq % 128 == 0 and ctx_len % 128 == 0
    mod_rows = ((nb + 1 + 7) // 8) * 8
    cond = jnp.zeros((mod_rows, D), F32).at[:nb].set(c).at[nb].set(c_ctx)
    lat_row = lambda b: b
    ctx_row = lambda b: nb

    cos128, sin128, sin_signed = _rope_tables(seq)
    ones_c = jnp.ones((ctx_len, LANE), F32)
    zeros_c = jnp.zeros((ctx_len, LANE), F32)
    x2d = x.reshape(n, D)
    c2d = ctx.reshape(nc, D)

    mods = _modulation(cond, ev_mod_w[0].astype(BF16), ev_mod_b[0])
    sh1, sc1, g1, sh2, sc2, g2 = [m.reshape(mod_rows, 1, D) for m in jnp.split(mods, N_MOD, axis=-1)]
    ng = ev_norm_g[0].reshape(4, 1, D)
    w_in_e, wq_main, wq_swap, wkv = _even_weights(ev_w_in[0], ev_w_uq[0], ev_w_ukv[0])
    qg = ev_q_norm_g[0].reshape(1, MLA_QR)
    kvg = ev_kv_norm_g[0].reshape(1, MLA_KVR)
    w_out_e = ev_w_out[0].astype(BF16)
    wg_e, wu_e, wd_e = ev_ffn_gate[0].astype(BF16), ev_ffn_up[0].astype(BF16), ev_ffn_down[0].astype(BF16)

    bg_l, uc_l, q_l, k_l, v_l = _front_even(x2d, sh1, sc1, lat_row, ng[0], w_in_e, qg, wq_main, wq_swap, kvg, wkv,
                                            cos128, sin128, seq)
    bg_c, uc_c, q_c, k_c, v_c = _front_even(c2d, sh1, sc1, ctx_row, ng[0], w_in_e, qg, wq_main, wq_swap, kvg, wkv,
                                            ones_c, zeros_c, ctx_len)
    at_l = _mla_attn(q_l, k_c, v_c, k_l, v_l, nb, seq, ctx_len)
    at_c = _mla_attn(q_c, k_c, v_c, None, None, nb, ctx_len, ctx_len)
    x1_l, h2_l = _mix_even(bg_l, uc_l, ev_conv_w[0], at_l, w_out_e, x2d, g1, ng[1], ng[2], sh2, sc2, lat_row, seq)
    x1_c, h2_c = _mix_even(bg_c, uc_c, ev_conv_w[0], at_c, w_out_e, c2d, g1, ng[1], ng[2], sh2, sc2, ctx_row,
                           ctx_len)
    x2d = _ffn_dense(h2_l, wg_e, wu_e, wd_e, x1_l, g2, ng[3], lat_row, seq)
    c2d = _ffn_dense(h2_c, wg_e, wu_e, wd_e, x1_c, g2, ng[3], ctx_row, ctx_len)

    lam_init = 0.8 - 0.6 * math.exp(-0.3 * 1)
    mods = _modulation(cond, od_mod_w[0].astype(BF16), od_mod_b[0])
    sh1, sc1, g1, sh2, sc2, g2 = [m.reshape(mod_rows, 1, D) for m in jnp.split(mods, N_MOD, axis=-1)]
    ng = od_norm_g[0].reshape(4, 1, D)
    w_in_o = od_w_in[0].astype(BF16)
    q_o, k_o, v_o, f_o = _front_odd(x2d, sh1, sc1, lat_row, ng[0], w_in_o, cos128, sin_signed, seq, True)
    kc_o, vc_o = _front_odd(c2d, sh1, sc1, ctx_row, ng[0], w_in_o[:, DIFF_W:3 * DIFF_W], ones_c, zeros_c, ctx_len,
                            False)
    ca = _diff_attn(q_o, kc_o, vc_o, k_o, v_o, od_lambda[0], od_subln_g[0].reshape(1, 2 * DIFF_HD), nb, seq,
                    ctx_len, lam_init)
    ct, st, cc, sc = _dft_tables(seq)
    fd = _fourier(f_o, ct, st, cc, sc, nb, seq)
    router_f = jnp.zeros((D, LANE), F32).at[:, :N_EXP].set(od_router[0])
    router_hi = router_f.astype(BF16)
    router_pad = jnp.concatenate([router_hi, (router_f - router_hi.astype(F32)).astype(BF16)], axis=1)
    x1, h2p, ridx, rwt = _mix_odd(ca, fd, od_w_out[0].astype(BF16), x2d, g1, ng[1], ng[2], sh2, sc2, router_pad,
                                  lat_row, seq)
    pos, tile_expert, tile_valid, rows = _route(ridx, n, TM_EXP)
    xs = _moe_scatter(tile_valid, pos, h2p, rows, TM_EXP)
    ys = _moe_experts(tile_expert, tile_valid, xs, od_exp_gate[0].astype(BF16), od_exp_up[0].astype(BF16),
                      od_exp_down[0].astype(BF16))
    out = _moe_combine(pos, ys, rwt, x1, g2, ng[3], lat_row, seq)
    return out.reshape(nb, seq, D)
```

```python
import functools
import math

import numpy as np
import jax
import jax.numpy as jnp
from jax import lax
from jax.experimental import pallas as pl
from jax.experimental.pallas import tpu as pltpu

F32 = jnp.float32
BF16 = jnp.bfloat16
I32 = jnp.int32
U32 = jnp.uint32

D = 1024
GRID_W = 64
EPS = 1e-6
ROPE_BASE = 10000.0
N_MOD = 6
SC_W = D // 2
MLA_V = 128
MLA_NOPE = 128
MLA_ROPE = 64
MLA_H = (D - SC_W) // MLA_V
MLA_QR = 3 * D // 8
MLA_KVR = D // 4
MLA_SCALE = (MLA_NOPE + MLA_ROPE) ** -0.5
DIFF_W = 3 * D // 4
DIFF_HD = 64
DIFF_H = DIFF_W // (2 * DIFF_HD)
DIFF_SCALE = DIFF_HD ** -0.5
FNET_W = D - DIFF_W
FNET_G = 4
FNET_GD = FNET_W // FNET_G
D_FF = ((8 * D // 3 + 127) // 128) * 128
N_EXP = 8
TOP_K = 2
LOG2E = math.log2(math.e)

LANE = 128
VMEM_LIMIT = 56 * 1024 * 1024

TM = 512
TQ_MLA = 512
TQ_DIFF = 256
BK_ATTN = 512
HP_DIFF = 3
TK_FFT = 512
TM_EXP = 512
N_DMA_GROUPS = 4


def _cp(sem, vmem=VMEM_LIMIT):
    return pltpu.CompilerParams(dimension_semantics=sem, vmem_limit_bytes=vmem)


def _rms(x, g):
    return x * lax.rsqrt(jnp.mean(x * x, axis=-1, keepdims=True) + EPS) * g


def _dot(a, b):
    return jnp.dot(a, b, preferred_element_type=F32)


def _dot_nt(a, b):
    return lax.dot_general(a, b, (((1,), (1,)), ((), ())), preferred_element_type=F32)


def _full(shape):
    nd = len(shape)
    return pl.BlockSpec(shape, lambda *_: (0,) * nd)


def _pack_bf16_pairs(x):
    n = x.shape[1] // 2
    hi = pltpu.bitcast(x[:, :n].astype(BF16).astype(F32), U32)
    lo = pltpu.bitcast(x[:, n:].astype(BF16).astype(F32), U32)
    return hi | (lo >> 16)


def _unpack_bf16_pairs(u):
    hi = pltpu.bitcast(u & jnp.uint32(0xFFFF0000), F32)
    lo = pltpu.bitcast(u << 16, F32)
    return jnp.concatenate([hi, lo], axis=1)


def _mod_kernel(c_ref, w_ref, b_ref, o_ref):
    c = c_ref[...]
    s = c / (1.0 + jnp.exp(-c))
    o_ref[...] = _dot(s.astype(BF16), w_ref[...]) + b_ref[...]


def _modulation(cond, w_bf, b):
    rows = cond.shape[0]
    n = w_bf.shape[1]
    tn = 1536
    return pl.pallas_call(
        _mod_kernel,
        grid=(n // tn,),
        in_specs=[_full((rows, D)), pl.BlockSpec((D, tn), lambda j: (0, j)), pl.BlockSpec((1, tn), lambda j: (0, j))],
        out_specs=pl.BlockSpec((rows, tn), lambda j: (0, j)),
        out_shape=jax.ShapeDtypeStruct((rows, n), F32),
        compiler_params=_cp(("arbitrary",)),
        name="modulation",
    )(cond, w_bf, b.reshape(1, n))


def _front_even_kernel(x_ref, sh_ref, sc_ref, g0_ref, win_ref, qg_ref, wq_ref, wqs_ref, kvg_ref, wkv_ref,
                       cos_ref, sin_ref, bg_ref, uc_ref, q_ref, k_ref, v_ref):
    x = x_ref[...]
    h = _rms(x, g0_ref[...]) * (1.0 + sc_ref[0]) + sh_ref[0]
    z = _dot(h.astype(BF16), win_ref[...])
    bg_ref[...] = z[:, 0:SC_W].astype(BF16)
    uc_ref[...] = (z[:, SC_W:2 * SC_W] * z[:, 2 * SC_W:3 * SC_W]).astype(BF16)
    o = 3 * SC_W
    zq = z[:, o:o + MLA_QR]
    zkv = z[:, o + MLA_QR:o + MLA_QR + MLA_KVR]
    o2 = o + MLA_QR + MLA_KVR
    kpe = z[:, o2:o2 + LANE]
    kpes = z[:, o2 + LANE:o2 + 2 * LANE]
    cos = cos_ref[...]
    sin = sin_ref[...]
    zqn = _rms(zq, qg_ref[...]).astype(BF16)
    qm = _dot(zqn, wq_ref[...])
    qs = _dot(zqn, wqs_ref[...])
    qscale = MLA_SCALE * LOG2E
    for hd in range(MLA_H):
        lo = qm[:, 256 * hd:256 * hd + LANE]
        hi = qm[:, 256 * hd + LANE:256 * hd + 2 * LANE] * cos + qs[:, LANE * hd:LANE * hd + LANE] * sin
        q_ref[hd, :, 0:LANE] = (lo * qscale).astype(BF16)
        q_ref[hd, :, LANE:2 * LANE] = (hi * qscale).astype(BF16)
    zkvn = _rms(zkv, kvg_ref[...]).astype(BF16)
    kv = _dot(zkvn, wkv_ref[...])
    kpr = (kpe * cos + kpes * sin).astype(BF16)
    for hd in range(MLA_H):
        k_ref[hd, :, 0:LANE] = kv[:, 256 * hd:256 * hd + LANE].astype(BF16)
        k_ref[hd, :, LANE:2 * LANE] = kpr
        v_ref[hd] = kv[:, 256 * hd + LANE:256 * hd + 2 * LANE].astype(BF16)


def _front_even(x2d, sh, sc, row_of, g0, w_in, qg, wq, wqs, kvg, wkv, cos, sin, seq):
    n = x2d.shape[0]
    tm = min(TM, seq)
    nper = seq // tm
    rowspec = pl.BlockSpec((1, 1, D), lambda i: (row_of(i // nper), 0, 0))
    tabspec = pl.BlockSpec((tm, LANE), lambda i: (i % nper, 0))
    win_n = w_in.shape[1]
    return pl.pallas_call(
        _front_even_kernel,
        grid=(n // tm,),
        in_specs=[pl.BlockSpec((tm, D), lambda i: (i, 0)), rowspec, rowspec, _full((1, D)), _full((D, win_n)),
                  _full((1, MLA_QR)), _full((MLA_QR, 4 * 256)), _full((MLA_QR, 4 * LANE)),
                  _full((1, MLA_KVR)), _full((MLA_KVR, 4 * 256)), tabspec, tabspec],
        out_specs=[pl.BlockSpec((tm, SC_W), lambda i: (i, 0)), pl.BlockSpec((tm, SC_W), lambda i: (i, 0)),
                   pl.BlockSpec((MLA_H, tm, 256), lambda i: (0, i, 0)),
                   pl.BlockSpec((MLA_H, tm, 256), lambda i: (0, i, 0)),
                   pl.BlockSpec((MLA_H, tm, LANE), lambda i: (0, i, 0))],
        out_shape=[jax.ShapeDtypeStruct((n, SC_W), BF16), jax.ShapeDtypeStruct((n, SC_W), BF16),
                   jax.ShapeDtypeStruct((MLA_H, n, 256), BF16), jax.ShapeDtypeStruct((MLA_H, n, 256), BF16),
                   jax.ShapeDtypeStruct((MLA_H, n, LANE), BF16)],
        compiler_params=_cp(("parallel",)),
        name="front_even",
    )(x2d, sh, sc, g0, w_in, qg, wq, wqs, kvg, wkv, cos, sin)


def _key_blocks(k_ref, v_ref, bk):
    n = k_ref.shape[0]
    return [(k_ref.at[pl.ds(j, min(bk, n - j))], v_ref.at[pl.ds(j, min(bk, n - j))]) for j in range(0, n, bk)]


def _online_softmax_pv(q, blocks):
    m = acc = None
    dv = blocks[0][1].shape[1]
    for k_blk, v_blk in blocks:
        bk = k_blk.shape[0]
        ones_col = (lax.broadcasted_iota(I32, (bk, LANE), 1) == 0).astype(BF16)
        v_ext = jnp.concatenate([v_blk[...], ones_col], axis=1)
        s = _dot_nt(q, k_blk[...])
        mb = jnp.max(s, axis=-1, keepdims=True)
        if m is None:
            m = mb
            acc = _dot(jnp.exp2(s - m).astype(BF16), v_ext)
        else:
            m_new = jnp.maximum(m, mb)
            acc = jnp.exp2(m - m_new) * acc + _dot(jnp.exp2(s - m_new).astype(BF16), v_ext)
            m = m_new
    return acc[:, :dv], acc[:, dv:dv + 1]


def _mla_attn_kernel(*refs, with_lat, bk):
    if with_lat:
        q_ref, kc_ref, vc_ref, kl_ref, vl_ref, o_ref = refs
    else:
        q_ref, kc_ref, vc_ref, o_ref = refs
    blocks = _key_blocks(kc_ref.at[0], vc_ref.at[0], bk)
    if with_lat:
        blocks += _key_blocks(kl_ref.at[0], vl_ref.at[0], bk)
    acc, l = _online_softmax_pv(q_ref[0], blocks)
    o_ref[...] = (acc * (1.0 / l)).astype(BF16)


def _mla_attn(q, k_ctx, v_ctx, k_lat, v_lat, nb, seq_q, ctx_len):
    with_lat = k_lat is not None
    tq = min(TQ_MLA, seq_q)
    nq = seq_q // tq
    in_specs = [pl.BlockSpec((1, tq, 256), lambda b, h, i: (h, b * nq + i, 0)),
                pl.BlockSpec((1, ctx_len, 256), lambda b, h, i: (h, b, 0)),
                pl.BlockSpec((1, ctx_len, LANE), lambda b, h, i: (h, b, 0))]
    args = [q, k_ctx, v_ctx]
    if with_lat:
        in_specs += [pl.BlockSpec((1, seq_q, 256), lambda b, h, i: (h, b, 0)),
                     pl.BlockSpec((1, seq_q, LANE), lambda b, h, i: (h, b, 0))]
        args += [k_lat, v_lat]
    return pl.pallas_call(
        functools.partial(_mla_attn_kernel, with_lat=with_lat, bk=BK_ATTN),
        grid=(nb, MLA_H, nq),
        in_specs=in_specs,
        out_specs=pl.BlockSpec((tq, LANE), lambda b, h, i: (b * nq + i, h)),
        out_shape=jax.ShapeDtypeStruct((nb * seq_q, MLA_H * MLA_V), BF16),
        compiler_params=_cp(("parallel", "parallel", "arbitrary")),
        name="mla_attn_lat" if with_lat else "mla_attn_ctx",
    )(*args)


def _residual_and_h2(y, x_ref, g1_ref, n1_ref, n2_ref, sh2_ref, sc2_ref):
    x1 = x_ref[...] + g1_ref[0] * _rms(y, n1_ref[...])
    h2 = _rms(x1, n2_ref[...]) * (1.0 + sc2_ref[0]) + sh2_ref[0]
    return x1, h2


def _mix_even_kernel(bg_ref, uc_ref, ucp_ref, ucn_ref, cw_ref, at_ref, wo_ref, x_ref, g1_ref, n1_ref, n2_ref,
                     sh2_ref, sc2_ref, x1_ref, h2_ref, scr, *, nper):
    tm = uc_ref.shape[0]
    i = pl.program_id(0)
    ucf = uc_ref[...].astype(F32)
    first = (i % nper) == 0
    last = (i % nper) == nper - 1
    prev_row = jnp.where(first, 0.0, ucp_ref[7:8, :].astype(F32))
    next_row = jnp.where(last, 0.0, ucn_ref[0:1, :].astype(F32))
    scr[8:8 + tm, :] = ucf
    scr[7:8, :] = prev_row
    scr[8 + tm:9 + tm, :] = next_row
    up = scr[7:7 + tm, :]
    dn = scr[9:9 + tm, :]
    conv = cw_ref[0:1, :] * up + cw_ref[1:2, :] * ucf + cw_ref[2:3, :] * dn
    a = (bg_ref[...].astype(F32) * conv).astype(BF16)
    y = _dot(a, wo_ref[0:SC_W, :]) + _dot(at_ref[...], wo_ref[SC_W:D, :])
    x1, h2 = _residual_and_h2(y, x_ref, g1_ref, n1_ref, n2_ref, sh2_ref, sc2_ref)
    x1_ref[...] = x1
    h2_ref[...] = h2.astype(BF16)


def _mix_even(bg, uc, conv_w, attn, w_out, x2d, g1, n1, n2, sh2, sc2, row_of, seq):
    n = x2d.shape[0]
    tm = min(TM, seq)
    nper = seq // tm
    nb8 = n // 8
    rowspec = pl.BlockSpec((1, 1, D), lambda i: (row_of(i // nper), 0, 0))
    tile = lambda w: pl.BlockSpec((tm, w), lambda i: (i, 0))
    return pl.pallas_call(
        functools.partial(_mix_even_kernel, nper=nper),
        grid=(n // tm,),
        in_specs=[tile(SC_W), tile(SC_W),
                  pl.BlockSpec((8, SC_W), lambda i: (jnp.maximum(i * (tm // 8) - 1, 0), 0)),
                  pl.BlockSpec((8, SC_W), lambda i: (jnp.minimum((i + 1) * (tm // 8), nb8 - 1), 0)),
                  _full((3, SC_W)), tile(MLA_H * MLA_V), _full((D, D)), tile(D), rowspec, _full((1, D)),
                  _full((1, D)), rowspec, rowspec],
        out_specs=[tile(D), tile(D)],
        out_shape=[jax.ShapeDtypeStruct((n, D), F32), jax.ShapeDtypeStruct((n, D), BF16)],
        scratch_shapes=[pltpu.VMEM((tm + 16, SC_W), F32)],
        compiler_params=_cp(("parallel",)),
        name="mix_even",
    )(bg, uc, uc, uc, conv_w, attn, w_out, x2d, g1, n1, n2, sh2, sc2)


def _mix_odd_kernel(ca_ref, fd_ref, wo_ref, x_ref, g1_ref, n1_ref, n2_ref, sh2_ref, sc2_ref, rw_ref,
                    x1_ref, h2p_ref, ridx_ref, rwt_ref):
    y = _dot(ca_ref[...], wo_ref[0:DIFF_W, :]) + _dot(fd_ref[...], wo_ref[DIFF_W:D, :])
    x1, h2 = _residual_and_h2(y, x_ref, g1_ref, n1_ref, n2_ref, sh2_ref, sc2_ref)
    x1_ref[...] = x1
    h2p_ref[...] = _pack_bf16_pairs(h2)
    tm = h2.shape[0]
    hi = h2.astype(BF16)
    lo = (h2 - hi.astype(F32)).astype(BF16)
    r = _dot(jnp.concatenate([hi, lo], axis=0), rw_ref[...])
    logits = (r[:tm, :LANE] + r[:tm, LANE:]) + (r[tm:, :LANE] + r[tm:, LANE:])
    lane = lax.broadcasted_iota(I32, logits.shape, 1).astype(F32)
    neg = jnp.float32(-jnp.inf)
    s0 = jnp.where(lane < N_EXP, logits, neg)
    m1 = jnp.max(s0, axis=-1, keepdims=True)
    i1 = jnp.min(jnp.where(s0 == m1, lane, float(LANE)), axis=-1, keepdims=True)
    s1 = jnp.where(lane == i1, neg, s0)
    m2 = jnp.max(s1, axis=-1, keepdims=True)
    i2 = jnp.min(jnp.where(s1 == m2, lane, float(LANE)), axis=-1, keepdims=True)
    e = jnp.exp(m2 - m1)
    w1 = 1.0 / (1.0 + e)
    w2 = e * w1
    ridx_ref[...] = jnp.where(lane == 0.0, i1, jnp.where(lane == 1.0, i2, 0.0)).astype(I32)
    rwt_ref[...] = jnp.where(lane == 0.0, w1, jnp.where(lane == 1.0, w2, 0.0))


def _mix_odd(cattn, fd, w_out, x2d, g1, n1, n2, sh2, sc2, router_pad, row_of, seq):
    n = x2d.shape[0]
    tm = min(TM, seq)
    nper = seq // tm
    rowspec = pl.BlockSpec((1, 1, D), lambda i: (row_of(i // nper), 0, 0))
    tile = lambda w: pl.BlockSpec((tm, w), lambda i: (i, 0))
    return pl.pallas_call(
        _mix_odd_kernel,
        grid=(n // tm,),
        in_specs=[tile(DIFF_W), tile(FNET_W), _full((D, D)), tile(D), rowspec, _full((1, D)), _full((1, D)),
                  rowspec, rowspec, _full((D, 2 * LANE))],
        out_specs=[tile(D), tile(D // 2), tile(LANE), tile(LANE)],
        out_shape=[jax.ShapeDtypeStruct((n, D), F32), jax.ShapeDtypeStruct((n, D // 2), U32),
                   jax.ShapeDtypeStruct((n, LANE), I32), jax.ShapeDtypeStruct((n, LANE), F32)],
        compiler_params=_cp(("parallel",)),
        name="mix_odd",
    )(cattn, fd, w_out, x2d, g1, n1, n2, sh2, sc2, router_pad)


def _swiglu(h, wg, wu, wd):
    g = _dot(h, wg)
    u = _dot(h, wu)
    a = (g / (1.0 + jnp.exp(-g)) * u).astype(BF16)
    return _dot(a, wd)


def _ffn_kernel(h_ref, wg_ref, wu_ref, wd_ref, x_ref, g2_ref, n3_ref, o_ref):
    f = _swiglu(h_ref[...], wg_ref[...], wu_ref[...], wd_ref[...])
    o_ref[...] = x_ref[...] + g2_ref[0] * _rms(f, n3_ref[...])


def _ffn_dense(h2, wg, wu, wd, x1, g2, n3, row_of, seq):
    n = x1.shape[0]
    tm = min(TM, seq)
    nper = seq // tm
    rowspec = pl.BlockSpec((1, 1, D), lambda i: (row_of(i // nper), 0, 0))
    tile = lambda w: pl.BlockSpec((tm, w), lambda i: (i, 0))
    once = lambda shape: pl.BlockSpec(shape, lambda i: (0, 0), pipeline_mode=pl.Buffered(1))
    return pl.pallas_call(
        _ffn_kernel,
        grid=(n // tm,),
        in_specs=[tile(D), once((D, D_FF)), once((D, D_FF)), once((D_FF, D)), tile(D), rowspec, _full((1, D))],
        out_specs=tile(D),
        out_shape=jax.ShapeDtypeStruct((n, D), F32),
        compiler_params=_cp(("parallel",)),
        name="ffn_dense",
    )(h2, wg, wu, wd, x1, g2, n3)


def _rope_slab(x, cos, sin_signed, lane):
    swap = jnp.where((lane & 63) < 32, pltpu.roll(x, 96, 1), pltpu.roll(x, 32, 1))
    return x * cos + swap * sin_signed


def _front_odd_kernel(x_ref, sh_ref, sc_ref, g0_ref, win_ref, cos_ref, sin_ref, q_ref, k_ref, v_ref, f_ref,
                      *, with_q):
    x = x_ref[...]
    h = _rms(x, g0_ref[...]) * (1.0 + sc_ref[0]) + sh_ref[0]
    z = _dot(h.astype(BF16), win_ref[...])
    cos = cos_ref[...]
    sin = sin_ref[...]
    lane = lax.broadcasted_iota(I32, cos.shape, 1)
    off = DIFF_W if with_q else 0
    qscale = DIFF_SCALE * LOG2E
    for g in range(DIFF_W // LANE):
        sl = slice(LANE * g, LANE * g + LANE)
        if with_q:
            q_ref[:, sl] = (_rope_slab(z[:, sl], cos, sin, lane) * qscale).astype(BF16)
        ksl = slice(off + LANE * g, off + LANE * g + LANE)
        k_ref[:, sl] = _rope_slab(z[:, ksl], cos, sin, lane).astype(BF16)
    v_ref[...] = z[:, off + DIFF_W:off + 2 * DIFF_W].astype(BF16)
    if with_q:
        f_ref[...] = z[:, 3 * DIFF_W:3 * DIFF_W + FNET_W].astype(BF16)


def _front_odd(x2d, sh, sc, row_of, g0, w_in, cos, sin, seq, with_q):
    n = x2d.shape[0]
    tm = min(TM, seq)
    nper = seq // tm
    rowspec = pl.BlockSpec((1, 1, D), lambda i: (row_of(i // nper), 0, 0))
    tabspec = pl.BlockSpec((tm, LANE), lambda i: (i % nper, 0))
    tile = lambda w: pl.BlockSpec((tm, w), lambda i: (i, 0))
    if with_q:
        kern = functools.partial(_front_odd_kernel, with_q=True)
        out_specs = [tile(DIFF_W), tile(DIFF_W), tile(DIFF_W), tile(FNET_W)]
        out_shape = [jax.ShapeDtypeStruct((n, DIFF_W), BF16)] * 3 + [jax.ShapeDtypeStruct((n, FNET_W), BF16)]
    else:
        def kern(x_ref, sh_ref, sc_ref, g0_ref, win_ref, cos_ref, sin_ref, k_ref, v_ref):
            _front_odd_kernel(x_ref, sh_ref, sc_ref, g0_ref, win_ref, cos_ref, sin_ref, None, k_ref, v_ref, None,
                              with_q=False)
        out_specs = [tile(DIFF_W), tile(DIFF_W)]
        out_shape = [jax.ShapeDtypeStruct((n, DIFF_W), BF16)] * 2
    return pl.pallas_call(
        kern,
        grid=(n // tm,),
        in_specs=[tile(D), rowspec, rowspec, _full((1, D)), _full((D, w_in.shape[1])), tabspec, tabspec],
        out_specs=out_specs,
        out_shape=out_shape,
        compiler_params=_cp(("parallel",)),
        name="front_odd" if with_q else "front_odd_ctx",
    )(x2d, sh, sc, g0, w_in, cos, sin)


def _diff_attn_kernel(q_ref, kc_ref, vc_ref, kl_ref, vl_ref, lam_ref, sg_ref, o_ref, *, lam_init, bk):
    tq = q_ref.shape[0]
    lp = lam_ref[...]
    lam = (jnp.exp(jnp.sum(lp[0:1, :] * lp[1:2, :], axis=-1, keepdims=True))
           - jnp.exp(jnp.sum(lp[2:3, :] * lp[3:4, :], axis=-1, keepdims=True)) + lam_init)
    lane = lax.broadcasted_iota(I32, (tq, LANE), 1)
    for hd in range(q_ref.shape[1] // LANE):
        sl = pl.ds(hd * LANE, LANE)
        q = q_ref[:, sl]
        zero = jnp.zeros_like(q)
        qq = jnp.concatenate([jnp.where(lane < DIFF_HD, q, zero), jnp.where(lane >= DIFF_HD, q, zero)], axis=0)
        blocks = (_key_blocks(kc_ref.at[:, sl], vc_ref.at[:, sl], bk)
                  + _key_blocks(kl_ref.at[:, sl], vl_ref.at[:, sl], bk))
        acc, l = _online_softmax_pv(qq, blocks)
        r = 1.0 / l
        o = acc[:tq] * r[:tq] - acc[tq:] * (r[tq:] * lam)
        o_ref[:, sl] = (_rms(o, sg_ref[...]) * (1.0 - lam_init)).astype(BF16)


def _diff_attn(q, k_ctx, v_ctx, k_lat, v_lat, lam_p, subln_g, nb, seq, ctx_len, lam_init):
    tq = min(TQ_DIFF, seq)
    nq = seq // tq
    return pl.pallas_call(
        functools.partial(_diff_attn_kernel, lam_init=lam_init, bk=BK_ATTN),
        grid=(nb, DIFF_H // HP_DIFF, nq),
        in_specs=[pl.BlockSpec((tq, HP_DIFF * LANE), lambda b, h, i: (b * nq + i, h)),
                  pl.BlockSpec((ctx_len, HP_DIFF * LANE), lambda b, h, i: (b, h)),
                  pl.BlockSpec((ctx_len, HP_DIFF * LANE), lambda b, h, i: (b, h)),
                  pl.BlockSpec((seq, HP_DIFF * LANE), lambda b, h, i: (b, h)),
                  pl.BlockSpec((seq, HP_DIFF * LANE), lambda b, h, i: (b, h)),
                  _full((4, DIFF_HD)), _full((1, 2 * DIFF_HD))],
        out_specs=pl.BlockSpec((tq, HP_DIFF * LANE), lambda b, h, i: (b * nq + i, h)),
        out_shape=jax.ShapeDtypeStruct((nb * seq, DIFF_W), BF16),
        compiler_params=_cp(("parallel", "parallel", "arbitrary")),
        name="diff_attn",
    )(q, k_ctx, v_ctx, k_lat, v_lat, lam_p, subln_g)


def _fourier_kernel(ct_ref, st_ref, f_ref, cc_ref, sc_ref, o_ref):
    f = f_ref[...]
    p = _dot(ct_ref[...], f).astype(BF16)
    q = _dot(st_ref[...], f).astype(BF16)
    o_ref[...] = (_dot(p, cc_ref[...]) - _dot(q, sc_ref[...])).astype(BF16)


def _fourier(f2d, ct, st, cc, sc, nb, seq):
    tk = min(TK_FFT, seq)
    nk = seq // tk
    return pl.pallas_call(
        _fourier_kernel,
        grid=(nk, nb),
        in_specs=[pl.BlockSpec((tk, seq), lambda j, b: (j, 0)), pl.BlockSpec((tk, seq), lambda j, b: (j, 0)),
                  pl.BlockSpec((seq, FNET_W), lambda j, b: (b, 0)), _full((FNET_W, FNET_W)), _full((FNET_W, FNET_W))],
        out_specs=pl.BlockSpec((tk, FNET_W), lambda j, b: (b * nk + j, 0)),
        out_shape=jax.ShapeDtypeStruct((nb * seq, FNET_W), BF16),
        compiler_params=_cp(("arbitrary", "arbitrary")),
        name="fourier",
    )(ct, st, f2d, cc, sc)


def _swiglu_chunks(x, wg_ref, wu_ref, wd_ref, n_chunks, between):
    f = wg_ref.shape[2]
    step = -(-f // (n_chunks * 256)) * 256
    acc = None
    for c in range(n_chunks):
        lo, hi = c * step, min((c + 1) * step, f)
        g = _dot(x, wg_ref[0, :, lo:hi])
        u = _dot(x, wu_ref[0, :, lo:hi])
        a = (g / (1.0 + jnp.exp(-g)) * u).astype(BF16)
        part = _dot(a, wd_ref[0, lo:hi, :])
        acc = part if acc is None else acc + part
        between(c)
    return acc


def _moe_expert_kernel(te_ref, tv_ref, idx_ref, idxn_ref, idxp_ref, h_ref, wg_ref, wu_ref, wd_ref, out_ref,
                       xbuf, ybuf, gsem, ssem, *, n_tok, n_chunks):
    tm = xbuf.shape[1]
    i = pl.program_id(0)
    n = pl.num_programs(0)
    cur = i % 2
    nvalid = tv_ref[i]
    nprev = jnp.where(i >= 1, tv_ref[jnp.maximum(i - 1, 0)], 0)
    next_used = jnp.logical_and(i + 1 < n, tv_ref[jnp.minimum(i + 1, n - 1)] > 0)

    def gather_row(ids_ref, b, r):
        j = ids_ref[0, 0, r]
        tok = jnp.where(j >= n_tok, j - n_tok, j)
        return pltpu.make_async_copy(h_ref.at[pl.ds(tok, 1)], xbuf.at[b, pl.ds(r, 1)], gsem.at[b])

    def scatter_row(ids_ref, b, r):
        return pltpu.make_async_copy(ybuf.at[b, pl.ds(r, 1)], out_ref.at[pl.ds(ids_ref[0, 0, r], 1)], ssem.at[b])

    def scatter_loop(ids_ref, b, nv):
        def issue(r, c):
            @pl.when(r < nv)
            def _():
                scatter_row(ids_ref, b, r).start()
            return c
        lax.fori_loop(0, tm, issue, 0)

    def wait_scatter(k):
        nv = tv_ref[k]
        b = k % 2

        @pl.when(nv == tm)
        def _():
            pltpu.make_async_copy(ybuf.at[b], out_ref.at[pl.ds(0, tm)], ssem.at[b]).wait()

        @pl.when(jnp.logical_and(nv > 0, nv < tm))
        def _():
            def drain(r, c):
                @pl.when(r < nv)
                def _():
                    pltpu.make_async_copy(ybuf.at[b, pl.ds(r, 1)], out_ref.at[pl.ds(0, 1)], ssem.at[b]).wait()
                return c
            lax.fori_loop(0, tm, drain, 0, unroll=8)

    @pl.when(jnp.logical_and(i == 0, nvalid > 0))
    def _():
        def issue(r, c):
            gather_row(idx_ref, 0, r).start()
            return c
        lax.fori_loop(0, tm, issue, 0)

    @pl.when(i >= 2)
    def _():
        wait_scatter(i - 2)

    @pl.when(nvalid > 0)
    def _():
        pltpu.make_async_copy(h_ref.at[pl.ds(0, tm)], xbuf.at[cur], gsem.at[cur]).wait()
        x = _unpack_bf16_pairs(xbuf[cur]).astype(BF16)

        def between(c):
            for r in range(c * tm // n_chunks, (c + 1) * tm // n_chunks):
                @pl.when(next_used)
                def _():
                    gather_row(idxn_ref, 1 - cur, r).start()

                @pl.when(r < nprev)
                def _():
                    scatter_row(idxp_ref, 1 - cur, r).start()

        ybuf[cur] = _pack_bf16_pairs(_swiglu_chunks(x, wg_ref, wu_ref, wd_ref, n_chunks, between))

    @pl.when(jnp.logical_and(nvalid == 0, nprev > 0))
    def _():
        scatter_loop(idxp_ref, 1 - cur, nprev)

    @pl.when(i == n - 1)
    def _():
        @pl.when(nvalid > 0)
        def _():
            scatter_loop(idx_ref, cur, nvalid)

        @pl.when(i >= 1)
        def _():
            wait_scatter(i - 1)
        wait_scatter(i)


def _moe_experts(tile_expert, tile_valid, ids, h2p, wg, wu, wd):
    n_tiles, _, tm = ids.shape
    n_tok = h2p.shape[0]
    once = lambda shape: pl.BlockSpec(shape, lambda i, te, tv: (te[i], 0, 0), pipeline_mode=pl.Buffered(1))
    grid_spec = pltpu.PrefetchScalarGridSpec(
        num_scalar_prefetch=2,
        grid=(n_tiles,),
        in_specs=[pl.BlockSpec((1, 1, tm), lambda i, te, tv: (i, 0, 0), memory_space=pltpu.SMEM),
                  pl.BlockSpec((1, 1, tm), lambda i, te, tv: (jnp.minimum(i + 1, n_tiles - 1), 0, 0),
                               memory_space=pltpu.SMEM),
                  pl.BlockSpec((1, 1, tm), lambda i, te, tv: (jnp.maximum(i - 1, 0), 0, 0), memory_space=pltpu.SMEM),
                  pl.BlockSpec(memory_space=pl.ANY),
                  once((1, D, D_FF)), once((1, D, D_FF)), once((1, D_FF, D))],
        out_specs=pl.BlockSpec(memory_space=pl.ANY),
        scratch_shapes=[pltpu.VMEM((2, tm, D // 2), U32), pltpu.VMEM((2, tm, D // 2), U32),
                        pltpu.SemaphoreType.DMA((2,)), pltpu.SemaphoreType.DMA((2,))],
    )
    return pl.pallas_call(
        functools.partial(_moe_expert_kernel, n_tok=n_tok, n_chunks=N_DMA_GROUPS),
        grid_spec=grid_spec,
        out_shape=jax.ShapeDtypeStruct((TOP_K * n_tok, D // 2), U32),
        compiler_params=_cp(("arbitrary",)),
        name="moe_experts",
    )(tile_expert, tile_valid, ids, ids, ids, h2p, wg, wu, wd)


def _moe_combine_kernel(y0_ref, y1_ref, rwt_ref, x_ref, g2_ref, n3_ref, o_ref):
    w = rwt_ref[...]
    f = _unpack_bf16_pairs(y0_ref[...]) * w[:, 0:1] + _unpack_bf16_pairs(y1_ref[...]) * w[:, 1:2]
    o_ref[...] = x_ref[...] + g2_ref[0] * _rms(f, n3_ref[...])


def _moe_combine(ys, rwt, x1, g2, n3, row_of, seq):
    n = x1.shape[0]
    ts = min(TM, seq)
    nper = seq // ts
    nsteps = n // ts
    rowspec = pl.BlockSpec((1, 1, D), lambda i: (row_of(i // nper), 0, 0))
    return pl.pallas_call(
        _moe_combine_kernel,
        grid=(nsteps,),
        in_specs=[pl.BlockSpec((ts, D // 2), lambda i: (i, 0)), pl.BlockSpec((ts, D // 2), lambda i: (i + nsteps, 0)),
                  pl.BlockSpec((ts, LANE), lambda i: (i, 0)), pl.BlockSpec((ts, D), lambda i: (i, 0)),
                  rowspec, _full((1, D))],
        out_specs=pl.BlockSpec((ts, D), lambda i: (i, 0)),
        out_shape=jax.ShapeDtypeStruct((n, D), F32),
        compiler_params=_cp(("arbitrary",)),
        name="moe_combine",
    )(ys, ys, rwt, x1, g2, n3)


def _route(ridx, n, tm):
    n_asg = TOP_K * n
    e_flat = jnp.concatenate([ridx[:, 0], ridx[:, 1]])
    counts = jnp.sum((e_flat[:, None] == jnp.arange(N_EXP, dtype=I32)[None, :]).astype(I32), axis=0)
    ptiles = (counts + tm - 1) // tm
    pad = ptiles * tm - counts
    fill_e = jnp.repeat(jnp.arange(N_EXP, dtype=I32), tm)
    fill_k = jnp.tile(jnp.arange(tm, dtype=I32), N_EXP)
    fill_key = jnp.where(fill_k < pad[fill_e], fill_e, N_EXP)
    keys = jnp.concatenate([e_flat, fill_key])
    vals = jnp.concatenate([jnp.arange(n_asg, dtype=I32), jnp.zeros((N_EXP * tm,), I32)])
    _, ids = lax.sort((keys, vals), num_keys=1, is_stable=True)
    n_tiles = n_asg // tm + N_EXP
    tile_end = jnp.cumsum(ptiles)
    t = jnp.arange(n_tiles, dtype=I32)
    te = jnp.sum((t[:, None] >= tile_end[None, :]).astype(I32), axis=1)
    used = te < N_EXP
    last_e = jnp.max(jnp.where(counts > 0, jnp.arange(N_EXP, dtype=I32), 0))
    te_c = jnp.where(used, te, last_e).astype(I32)
    start = jnp.sum(jnp.where(t[:, None] >= tile_end[None, :], ptiles[None, :], 0), axis=1)
    left = counts[jnp.minimum(te, N_EXP - 1)] - (t - start) * tm
    tv = jnp.where(used, jnp.clip(left, 0, tm), 0).astype(I32)
    return ids.reshape(n_tiles, 1, tm), te_c, tv


def _rope_tables(seq):
    rows = seq // GRID_W
    row = np.repeat(np.arange(rows, dtype=np.float64), GRID_W)
    col = np.tile(np.arange(GRID_W, dtype=np.float64), rows)
    n_freq = MLA_ROPE // 4
    inv = ROPE_BASE ** (-np.arange(n_freq, dtype=np.float64) / n_freq)
    ang = np.concatenate([row[:, None] * inv, col[:, None] * inv], axis=-1)
    cos, sin = np.cos(ang), np.sin(ang)
    cos128 = np.tile(cos, (1, 4)).astype(np.float32)
    sin128 = np.tile(sin, (1, 4)).astype(np.float32)
    sin_signed = np.tile(np.concatenate([-sin, sin], axis=-1), (1, 2)).astype(np.float32)
    return jnp.asarray(cos128), jnp.asarray(sin128), jnp.asarray(sin_signed)


def _dft_tables(seq):
    k = np.arange(seq, dtype=np.int64)
    ang = 2.0 * np.pi * ((k[:, None] * k[None, :]) % seq).astype(np.float64) / seq
    ct = (np.cos(ang) / np.sqrt(seq)).astype(np.float32)
    st = (np.sin(ang) / np.sqrt(seq)).astype(np.float32)
    c = np.arange(FNET_GD, dtype=np.int64)
    angc = 2.0 * np.pi * ((c[:, None] * c[None, :]) % FNET_GD).astype(np.float64) / FNET_GD
    eye = np.eye(FNET_G)
    cc = np.kron(eye, np.cos(angc) / np.sqrt(FNET_GD)).astype(np.float32)
    sc = np.kron(eye, np.sin(angc) / np.sqrt(FNET_GD)).astype(np.float32)
    return (jnp.asarray(ct, dtype=BF16), jnp.asarray(st, dtype=BF16), jnp.asarray(cc, dtype=BF16),
            jnp.asarray(sc, dtype=BF16))


def _rot_half_cols(w):
    return jnp.concatenate([-w[..., MLA_ROPE // 2:], w[..., :MLA_ROPE // 2]], axis=-1)


def _even_weights(w_in, w_uq, w_ukv):
    o = 3 * SC_W + MLA_QR + MLA_KVR
    kpe = w_in[:, o:o + MLA_ROPE]
    z64 = jnp.zeros((D, LANE - MLA_ROPE), w_in.dtype)
    w_in_ext = jnp.concatenate([w_in[:, :o], kpe, z64, _rot_half_cols(kpe), z64], axis=1).astype(BF16)
    wq = w_uq.reshape(MLA_QR, MLA_H, MLA_NOPE + MLA_ROPE)
    zq = jnp.zeros((MLA_QR, MLA_H, LANE - MLA_ROPE), w_uq.dtype)
    wq_main = jnp.concatenate([wq, zq], axis=-1).reshape(MLA_QR, MLA_H * 256).astype(BF16)
    wq_swap = jnp.concatenate([_rot_half_cols(wq[..., MLA_NOPE:]), zq], axis=-1).reshape(MLA_QR, MLA_H * LANE)
    return w_in_ext, wq_main, wq_swap.astype(BF16), w_ukv.astype(BF16)


def kernel(x, c, ctx, c_ctx, ev_mod_w, ev_mod_b, ev_norm_g, ev_w_in, ev_conv_w, ev_q_norm_g, ev_w_uq, ev_kv_norm_g,
           ev_w_ukv, ev_w_out, ev_ffn_gate, ev_ffn_up, ev_ffn_down, od_mod_w, od_mod_b, od_norm_g, od_w_in,
           od_lambda, od_subln_g, od_w_out, od_router, od_exp_gate, od_exp_up, od_exp_down):
    nb, seq, _ = x.shape
    ctx_len = ctx.shape[1]
    n = nb * seq
    nc = nb * ctx_len
    assert seq % GRID_W == 0 and seq % 128 == 0 and ctx_len % 128 == 0
    mod_rows = ((nb + 1 + 7) // 8) * 8
    cond = jnp.zeros((mod_rows, D), F32).at[:nb].set(c).at[nb].set(c_ctx)
    lat_row = lambda b: b
    ctx_row = lambda b: nb

    cos128, sin128, sin_signed = _rope_tables(seq)
    ones_c = jnp.ones((ctx_len, LANE), F32)
    zeros_c = jnp.zeros((ctx_len, LANE), F32)
    x2d = x.reshape(n, D)
    c2d = ctx.reshape(nc, D)

    mods = _modulation(cond, ev_mod_w[0].astype(BF16), ev_mod_b[0])
    sh1, sc1, g1, sh2, sc2, g2 = [m.reshape(mod_rows, 1, D) for m in jnp.split(mods, N_MOD, axis=-1)]
    ng = ev_norm_g[0].reshape(4, 1, D)
    w_in_e, wq_main, wq_swap, wkv = _even_weights(ev_w_in[0], ev_w_uq[0], ev_w_ukv[0])
    qg = ev_q_norm_g[0].reshape(1, MLA_QR)
    kvg = ev_kv_norm_g[0].reshape(1, MLA_KVR)
    w_out_e = ev_w_out[0].astype(BF16)
    wg_e, wu_e, wd_e = ev_ffn_gate[0].astype(BF16), ev_ffn_up[0].astype(BF16), ev_ffn_down[0].astype(BF16)

    bg_l, uc_l, q_l, k_l, v_l = _front_even(x2d, sh1, sc1, lat_row, ng[0], w_in_e, qg, wq_main, wq_swap, kvg, wkv,
                                            cos128, sin128, seq)
    bg_c, uc_c, q_c, k_c, v_c = _front_even(c2d, sh1, sc1, ctx_row, ng[0], w_in_e, qg, wq_main, wq_swap, kvg, wkv,
                                            ones_c, zeros_c, ctx_len)
    at_l = _mla_attn(q_l, k_c, v_c, k_l, v_l, nb, seq, ctx_len)
    at_c = _mla_attn(q_c, k_c, v_c, None, None, nb, ctx_len, ctx_len)
    x1_l, h2_l = _mix_even(bg_l, uc_l, ev_conv_w[0], at_l, w_out_e, x2d, g1, ng[1], ng[2], sh2, sc2, lat_row, seq)
    x1_c, h2_c = _mix_even(bg_c, uc_c, ev_conv_w[0], at_c, w_out_e, c2d, g1, ng[1], ng[2], sh2, sc2, ctx_row,
                           ctx_len)
    x2d = _ffn_dense(h2_l, wg_e, wu_e, wd_e, x1_l, g2, ng[3], lat_row, seq)
    c2d = _ffn_dense(h2_c, wg_e, wu_e, wd_e, x1_c, g2, ng[3], ctx_row, ctx_len)

    lam_init = 0.8 - 0.6 * math.exp(-0.3 * 1)
    mods = _modulation(cond, od_mod_w[0].astype(BF16), od_mod_b[0])
    sh1, sc1, g1, sh2, sc2, g2 = [m.reshape(mod_rows, 1, D) for m in jnp.split(mods, N_MOD, axis=-1)]
    ng = od_norm_g[0].reshape(4, 1, D)
    w_in_o = od_w_in[0].astype(BF16)
    q_o, k_o, v_o, f_o = _front_odd(x2d, sh1, sc1, lat_row, ng[0], w_in_o, cos128, sin_signed, seq, True)
    kc_o, vc_o = _front_odd(c2d, sh1, sc1, ctx_row, ng[0], w_in_o[:, DIFF_W:3 * DIFF_W], ones_c, zeros_c, ctx_len,
                            False)
    ca = _diff_attn(q_o, kc_o, vc_o, k_o, v_o, od_lambda[0], od_subln_g[0].reshape(1, 2 * DIFF_HD), nb, seq,
                    ctx_len, lam_init)
    ct, st, cc, sc = _dft_tables(seq)
    fd = _fourier(f_o, ct, st, cc, sc, nb, seq)
    router_f = jnp.zeros((D, LANE), F32).at[:, :N_EXP].set(od_router[0])
    router_hi = router_f.astype(BF16)
    router_pad = jnp.concatenate([router_hi, (router_f - router_hi.astype(F32)).astype(BF16)], axis=1)
    x1, h2p, ridx, rwt = _mix_odd(ca, fd, od_w_out[0].astype(BF16), x2d, g1, ng[1], ng[2], sh2, sc2, router_pad,
                                  lat_row, seq)
    ids, tile_expert, tile_valid = _route(ridx, n, TM_EXP)
    ys = _moe_experts(tile_expert, tile_valid, ids, h2p, od_exp_gate[0].astype(BF16), od_exp_up[0].astype(BF16),
                      od_exp_down[0].astype(BF16))
    out = _moe_combine(ys, rwt, x1, g2, ng[3], lat_row, seq)
    return out.reshape(nb, seq, D)
```

```python
import functools
import math

import numpy as np
import jax
import jax.numpy as jnp
from jax import lax
from jax.experimental import pallas as pl
from jax.experimental.pallas import tpu as pltpu

F32 = jnp.float32
BF16 = jnp.bfloat16
I32 = jnp.int32
U32 = jnp.uint32

D = 1024
GRID_W = 64
EPS = 1e-6
ROPE_BASE = 10000.0
N_MOD = 6
SC_W = D // 2
MLA_V = 128
MLA_NOPE = 128
MLA_ROPE = 64
MLA_H = (D - SC_W) // MLA_V
MLA_QR = 3 * D // 8
MLA_KVR = D // 4
MLA_SCALE = (MLA_NOPE + MLA_ROPE) ** -0.5
DIFF_W = 3 * D // 4
DIFF_HD = 64
DIFF_H = DIFF_W // (2 * DIFF_HD)
DIFF_SCALE = DIFF_HD ** -0.5
FNET_W = D - DIFF_W
FNET_G = 4
FNET_GD = FNET_W // FNET_G
D_FF = ((8 * D // 3 + 127) // 128) * 128
N_EXP = 8
TOP_K = 2
LOG2E = math.log2(math.e)

LANE = 128
VMEM_LIMIT = 56 * 1024 * 1024

TM = 512
TQ_MLA = 1024
TQ_DIFF = 256
BK_ATTN = 256
HP_DIFF = 3
TK_FFT = 512
TM_EXP = 512
N_DMA_GROUPS = 4


def _cp(sem, vmem=VMEM_LIMIT):
    return pltpu.CompilerParams(dimension_semantics=sem, vmem_limit_bytes=vmem)


def _rms(x, g):
    return x * lax.rsqrt(jnp.mean(x * x, axis=-1, keepdims=True) + EPS) * g


def _dot(a, b):
    return jnp.dot(a, b, preferred_element_type=F32)


def _dot_nt(a, b):
    return lax.dot_general(a, b, (((1,), (1,)), ((), ())), preferred_element_type=F32)


def _full(shape):
    nd = len(shape)
    return pl.BlockSpec(shape, lambda *_: (0,) * nd)


def _pack_bf16_pairs(x):
    n = x.shape[1] // 2
    hi = pltpu.bitcast(x[:, :n].astype(BF16).astype(F32), U32)
    lo = pltpu.bitcast(x[:, n:].astype(BF16).astype(F32), U32)
    return hi | (lo >> 16)


def _unpack_bf16_pairs(u):
    hi = pltpu.bitcast(u & jnp.uint32(0xFFFF0000), F32)
    lo = pltpu.bitcast(u << 16, F32)
    return jnp.concatenate([hi, lo], axis=1)


def _mod_kernel(c_ref, w_ref, b_ref, o_ref):
    c = c_ref[...]
    s = c / (1.0 + jnp.exp(-c))
    o_ref[...] = _dot(s.astype(BF16), w_ref[...]) + b_ref[...]


def _modulation(cond, w_bf, b):
    rows = cond.shape[0]
    n = w_bf.shape[1]
    tn = 1536
    return pl.pallas_call(
        _mod_kernel,
        grid=(n // tn,),
        in_specs=[_full((rows, D)), pl.BlockSpec((D, tn), lambda j: (0, j)), pl.BlockSpec((1, tn), lambda j: (0, j))],
        out_specs=pl.BlockSpec((rows, tn), lambda j: (0, j)),
        out_shape=jax.ShapeDtypeStruct((rows, n), F32),
        compiler_params=_cp(("arbitrary",)),
        name="modulation",
    )(cond, w_bf, b.reshape(1, n))


def _front_even_kernel(x_ref, sh_ref, sc_ref, g0_ref, win_ref, qg_ref, wq_ref, wqs_ref, kvg_ref, wkv_ref,
                       cos_ref, sin_ref, bg_ref, uc_ref, q_ref, k_ref, v_ref):
    x = x_ref[...]
    h = _rms(x, g0_ref[...]) * (1.0 + sc_ref[0]) + sh_ref[0]
    z = _dot(h.astype(BF16), win_ref[...])
    bg_ref[...] = z[:, 0:SC_W].astype(BF16)
    uc_ref[...] = (z[:, SC_W:2 * SC_W] * z[:, 2 * SC_W:3 * SC_W]).astype(BF16)
    o = 3 * SC_W
    zq = z[:, o:o + MLA_QR]
    zkv = z[:, o + MLA_QR:o + MLA_QR + MLA_KVR]
    o2 = o + MLA_QR + MLA_KVR
    kpe = z[:, o2:o2 + LANE]
    kpes = z[:, o2 + LANE:o2 + 2 * LANE]
    cos = cos_ref[...]
    sin = sin_ref[...]
    zqn = _rms(zq, qg_ref[...]).astype(BF16)
    qm = _dot(zqn, wq_ref[...])
    qs = _dot(zqn, wqs_ref[...])
    qscale = MLA_SCALE * LOG2E
    for hd in range(MLA_H):
        lo = qm[:, 256 * hd:256 * hd + LANE]
        hi = qm[:, 256 * hd + LANE:256 * hd + 2 * LANE] * cos + qs[:, LANE * hd:LANE * hd + LANE] * sin
        q_ref[hd, :, 0:LANE] = (lo * qscale).astype(BF16)
        q_ref[hd, :, LANE:2 * LANE] = (hi * qscale).astype(BF16)
    zkvn = _rms(zkv, kvg_ref[...]).astype(BF16)
    kv = _dot(zkvn, wkv_ref[...])
    kpr = (kpe * cos + kpes * sin).astype(BF16)
    for hd in range(MLA_H):
        k_ref[hd, :, 0:LANE] = kv[:, 256 * hd:256 * hd + LANE].astype(BF16)
        k_ref[hd, :, LANE:2 * LANE] = kpr
        v_ref[hd] = kv[:, 256 * hd + LANE:256 * hd + 2 * LANE].astype(BF16)


def _front_even(x2d, sh, sc, row_of, g0, w_in, qg, wq, wqs, kvg, wkv, cos, sin, seq):
    n = x2d.shape[0]
    tm = min(TM, seq)
    nper = seq // tm
    rowspec = pl.BlockSpec((1, 1, D), lambda i: (row_of(i // nper), 0, 0))
    tabspec = pl.BlockSpec((tm, LANE), lambda i: (i % nper, 0))
    win_n = w_in.shape[1]
    return pl.pallas_call(
        _front_even_kernel,
        grid=(n // tm,),
        in_specs=[pl.BlockSpec((tm, D), lambda i: (i, 0)), rowspec, rowspec, _full((1, D)), _full((D, win_n)),
                  _full((1, MLA_QR)), _full((MLA_QR, 4 * 256)), _full((MLA_QR, 4 * LANE)),
                  _full((1, MLA_KVR)), _full((MLA_KVR, 4 * 256)), tabspec, tabspec],
        out_specs=[pl.BlockSpec((tm, SC_W), lambda i: (i, 0)), pl.BlockSpec((tm, SC_W), lambda i: (i, 0)),
                   pl.BlockSpec((MLA_H, tm, 256), lambda i: (0, i, 0)),
                   pl.BlockSpec((MLA_H, tm, 256), lambda i: (0, i, 0)),
                   pl.BlockSpec((MLA_H, tm, LANE), lambda i: (0, i, 0))],
        out_shape=[jax.ShapeDtypeStruct((n, SC_W), BF16), jax.ShapeDtypeStruct((n, SC_W), BF16),
                   jax.ShapeDtypeStruct((MLA_H, n, 256), BF16), jax.ShapeDtypeStruct((MLA_H, n, 256), BF16),
                   jax.ShapeDtypeStruct((MLA_H, n, LANE), BF16)],
        compiler_params=_cp(("parallel",)),
        name="front_even",
    )(x2d, sh, sc, g0, w_in, qg, wq, wqs, kvg, wkv, cos, sin)


def _key_blocks(k_ref, v_ref, bk):
    n = k_ref.shape[0]
    return [(k_ref.at[pl.ds(j, min(bk, n - j))], v_ref.at[pl.ds(j, min(bk, n - j))]) for j in range(0, n, bk)]


def _online_softmax_pv(q, blocks):
    m = acc = None
    dv = blocks[0][1].shape[1]
    for k_blk, v_blk in blocks:
        bk = k_blk.shape[0]
        ones_col = (lax.broadcasted_iota(I32, (bk, LANE), 1) == 0).astype(BF16)
        v_ext = jnp.concatenate([v_blk[...], ones_col], axis=1)
        s = _dot_nt(q, k_blk[...])
        mb = jnp.max(s, axis=-1, keepdims=True)
        if m is None:
            m = mb
            acc = _dot(jnp.exp2(s - m).astype(BF16), v_ext)
        else:
            m_new = jnp.maximum(m, mb)
            acc = jnp.exp2(m - m_new) * acc + _dot(jnp.exp2(s - m_new).astype(BF16), v_ext)
            m = m_new
    return acc[:, :dv], acc[:, dv:dv + 1]


def _mla_attn_kernel(*refs, with_lat, bk):
    if with_lat:
        q_ref, kc_ref, vc_ref, kl_ref, vl_ref, o_ref = refs
    else:
        q_ref, kc_ref, vc_ref, o_ref = refs
    blocks = _key_blocks(kc_ref.at[0], vc_ref.at[0], bk)
    if with_lat:
        blocks += _key_blocks(kl_ref.at[0], vl_ref.at[0], bk)
    acc, l = _online_softmax_pv(q_ref[0], blocks)
    o_ref[...] = (acc * (1.0 / l)).astype(BF16)


def _mla_attn(q, k_ctx, v_ctx, k_lat, v_lat, nb, seq_q, ctx_len):
    with_lat = k_lat is not None
    tq = min(TQ_MLA, seq_q)
    nq = seq_q // tq
    in_specs = [pl.BlockSpec((1, tq, 256), lambda b, h, i: (h, b * nq + i, 0)),
                pl.BlockSpec((1, ctx_len, 256), lambda b, h, i: (h, b, 0)),
                pl.BlockSpec((1, ctx_len, LANE), lambda b, h, i: (h, b, 0))]
    args = [q, k_ctx, v_ctx]
    if with_lat:
        in_specs += [pl.BlockSpec((1, seq_q, 256), lambda b, h, i: (h, b, 0)),
                     pl.BlockSpec((1, seq_q, LANE), lambda b, h, i: (h, b, 0))]
        args += [k_lat, v_lat]
    return pl.pallas_call(
        functools.partial(_mla_attn_kernel, with_lat=with_lat, bk=BK_ATTN),
        grid=(nb, MLA_H, nq),
        in_specs=in_specs,
        out_specs=pl.BlockSpec((tq, LANE), lambda b, h, i: (b * nq + i, h)),
        out_shape=jax.ShapeDtypeStruct((nb * seq_q, MLA_H * MLA_V), BF16),
        compiler_params=_cp(("parallel", "parallel", "arbitrary")),
        name="mla_attn_lat" if with_lat else "mla_attn_ctx",
    )(*args)


def _residual_and_h2(y, x_ref, g1_ref, n1_ref, n2_ref, sh2_ref, sc2_ref):
    x1 = x_ref[...] + g1_ref[0] * _rms(y, n1_ref[...])
    h2 = _rms(x1, n2_ref[...]) * (1.0 + sc2_ref[0]) + sh2_ref[0]
    return x1, h2


def _mix_even_kernel(bg_ref, uc_ref, ucp_ref, ucn_ref, cw_ref, at_ref, wo_ref, x_ref, g1_ref, n1_ref, n2_ref,
                     sh2_ref, sc2_ref, x1_ref, h2_ref, scr, *, nper):
    tm = uc_ref.shape[0]
    i = pl.program_id(0)
    ucf = uc_ref[...].astype(F32)
    first = (i % nper) == 0
    last = (i % nper) == nper - 1
    prev_row = jnp.where(first, 0.0, ucp_ref[7:8, :].astype(F32))
    next_row = jnp.where(last, 0.0, ucn_ref[0:1, :].astype(F32))
    scr[8:8 + tm, :] = ucf
    scr[7:8, :] = prev_row
    scr[8 + tm:9 + tm, :] = next_row
    up = scr[7:7 + tm, :]
    dn = scr[9:9 + tm, :]
    conv = cw_ref[0:1, :] * up + cw_ref[1:2, :] * ucf + cw_ref[2:3, :] * dn
    a = (bg_ref[...].astype(F32) * conv).astype(BF16)
    y = _dot(a, wo_ref[0:SC_W, :]) + _dot(at_ref[...], wo_ref[SC_W:D, :])
    x1, h2 = _residual_and_h2(y, x_ref, g1_ref, n1_ref, n2_ref, sh2_ref, sc2_ref)
    x1_ref[...] = x1
    h2_ref[...] = h2.astype(BF16)


def _mix_even(bg, uc, conv_w, attn, w_out, x2d, g1, n1, n2, sh2, sc2, row_of, seq):
    n = x2d.shape[0]
    tm = min(TM, seq)
    nper = seq // tm
    nb8 = n // 8
    rowspec = pl.BlockSpec((1, 1, D), lambda i: (row_of(i // nper), 0, 0))
    tile = lambda w: pl.BlockSpec((tm, w), lambda i: (i, 0))
    return pl.pallas_call(
        functools.partial(_mix_even_kernel, nper=nper),
        grid=(n // tm,),
        in_specs=[tile(SC_W), tile(SC_W),
                  pl.BlockSpec((8, SC_W), lambda i: (jnp.maximum(i * (tm // 8) - 1, 0), 0)),
                  pl.BlockSpec((8, SC_W), lambda i: (jnp.minimum((i + 1) * (tm // 8), nb8 - 1), 0)),
                  _full((3, SC_W)), tile(MLA_H * MLA_V), _full((D, D)), tile(D), rowspec, _full((1, D)),
                  _full((1, D)), rowspec, rowspec],
        out_specs=[tile(D), tile(D)],
        out_shape=[jax.ShapeDtypeStruct((n, D), F32), jax.ShapeDtypeStruct((n, D), BF16)],
        scratch_shapes=[pltpu.VMEM((tm + 16, SC_W), F32)],
        compiler_params=_cp(("parallel",)),
        name="mix_even",
    )(bg, uc, uc, uc, conv_w, attn, w_out, x2d, g1, n1, n2, sh2, sc2)


def _mix_odd_kernel(ca_ref, fd_ref, wo_ref, x_ref, g1_ref, n1_ref, n2_ref, sh2_ref, sc2_ref, rw_ref,
                    x1_ref, h2p_ref, ridx_ref, rwt_ref):
    y = _dot(ca_ref[...], wo_ref[0:DIFF_W, :]) + _dot(fd_ref[...], wo_ref[DIFF_W:D, :])
    x1, h2 = _residual_and_h2(y, x_ref, g1_ref, n1_ref, n2_ref, sh2_ref, sc2_ref)
    x1_ref[...] = x1
    h2p_ref[...] = _pack_bf16_pairs(h2)
    tm = h2.shape[0]
    hi = h2.astype(BF16)
    lo = (h2 - hi.astype(F32)).astype(BF16)
    r = _dot(jnp.concatenate([hi, lo], axis=0), rw_ref[...])
    logits = (r[:tm, :LANE] + r[:tm, LANE:]) + (r[tm:, :LANE] + r[tm:, LANE:])
    lane = lax.broadcasted_iota(I32, logits.shape, 1).astype(F32)
    neg = jnp.float32(-jnp.inf)
    s0 = jnp.where(lane < N_EXP, logits, neg)
    m1 = jnp.max(s0, axis=-1, keepdims=True)
    i1 = jnp.min(jnp.where(s0 == m1, lane, float(LANE)), axis=-1, keepdims=True)
    s1 = jnp.where(lane == i1, neg, s0)
    m2 = jnp.max(s1, axis=-1, keepdims=True)
    i2 = jnp.min(jnp.where(s1 == m2, lane, float(LANE)), axis=-1, keepdims=True)
    e = jnp.exp(m2 - m1)
    w1 = 1.0 / (1.0 + e)
    w2 = e * w1
    ridx_ref[...] = jnp.where(lane == 0.0, i1, jnp.where(lane == 1.0, i2, 0.0)).astype(I32)
    rwt_ref[...] = jnp.where(lane == 0.0, w1, jnp.where(lane == 1.0, w2, 0.0))


def _mix_odd(cattn, fd, w_out, x2d, g1, n1, n2, sh2, sc2, router_pad, row_of, seq):
    n = x2d.shape[0]
    tm = min(TM, seq)
    nper = seq // tm
    rowspec = pl.BlockSpec((1, 1, D), lambda i: (row_of(i // nper), 0, 0))
    tile = lambda w: pl.BlockSpec((tm, w), lambda i: (i, 0))
    return pl.pallas_call(
        _mix_odd_kernel,
        grid=(n // tm,),
        in_specs=[tile(DIFF_W), tile(FNET_W), _full((D, D)), tile(D), rowspec, _full((1, D)), _full((1, D)),
                  rowspec, rowspec, _full((D, 2 * LANE))],
        out_specs=[tile(D), tile(D // 2), tile(LANE), tile(LANE)],
        out_shape=[jax.ShapeDtypeStruct((n, D), F32), jax.ShapeDtypeStruct((n, D // 2), U32),
                   jax.ShapeDtypeStruct((n, LANE), I32), jax.ShapeDtypeStruct((n, LANE), F32)],
        compiler_params=_cp(("parallel",)),
        name="mix_odd",
    )(cattn, fd, w_out, x2d, g1, n1, n2, sh2, sc2, router_pad)


def _swiglu(h, wg, wu, wd):
    g = _dot(h, wg)
    u = _dot(h, wu)
    a = (g / (1.0 + jnp.exp(-g)) * u).astype(BF16)
    return _dot(a, wd)


def _ffn_kernel(h_ref, wg_ref, wu_ref, wd_ref, x_ref, g2_ref, n3_ref, o_ref):
    f = _swiglu(h_ref[...], wg_ref[...], wu_ref[...], wd_ref[...])
    o_ref[...] = x_ref[...] + g2_ref[0] * _rms(f, n3_ref[...])


def _ffn_dense(h2, wg, wu, wd, x1, g2, n3, row_of, seq):
    n = x1.shape[0]
    tm = min(TM, seq)
    nper = seq // tm
    rowspec = pl.BlockSpec((1, 1, D), lambda i: (row_of(i // nper), 0, 0))
    tile = lambda w: pl.BlockSpec((tm, w), lambda i: (i, 0))
    once = lambda shape: pl.BlockSpec(shape, lambda i: (0, 0), pipeline_mode=pl.Buffered(1))
    return pl.pallas_call(
        _ffn_kernel,
        grid=(n // tm,),
        in_specs=[tile(D), once((D, D_FF)), once((D, D_FF)), once((D_FF, D)), tile(D), rowspec, _full((1, D))],
        out_specs=tile(D),
        out_shape=jax.ShapeDtypeStruct((n, D), F32),
        compiler_params=_cp(("parallel",)),
        name="ffn_dense",
    )(h2, wg, wu, wd, x1, g2, n3)


def _rope_slab(x, cos, sin_signed, lane):
    swap = jnp.where((lane & 63) < 32, pltpu.roll(x, 96, 1), pltpu.roll(x, 32, 1))
    return x * cos + swap * sin_signed


def _front_odd_kernel(x_ref, sh_ref, sc_ref, g0_ref, win_ref, cos_ref, sin_ref, q_ref, k_ref, v_ref, f_ref,
                      *, with_q):
    x = x_ref[...]
    h = _rms(x, g0_ref[...]) * (1.0 + sc_ref[0]) + sh_ref[0]
    z = _dot(h.astype(BF16), win_ref[...])
    cos = cos_ref[...]
    sin = sin_ref[...]
    lane = lax.broadcasted_iota(I32, cos.shape, 1)
    off = DIFF_W if with_q else 0
    qscale = DIFF_SCALE * LOG2E
    for g in range(DIFF_W // LANE):
        sl = slice(LANE * g, LANE * g + LANE)
        if with_q:
            q_ref[:, sl] = (_rope_slab(z[:, sl], cos, sin, lane) * qscale).astype(BF16)
        ksl = slice(off + LANE * g, off + LANE * g + LANE)
        k_ref[:, sl] = _rope_slab(z[:, ksl], cos, sin, lane).astype(BF16)
    v_ref[...] = z[:, off + DIFF_W:off + 2 * DIFF_W].astype(BF16)
    if with_q:
        f_ref[...] = z[:, 3 * DIFF_W:3 * DIFF_W + FNET_W].astype(BF16)


def _front_odd(x2d, sh, sc, row_of, g0, w_in, cos, sin, seq, with_q):
    n = x2d.shape[0]
    tm = min(TM, seq)
    nper = seq // tm
    rowspec = pl.BlockSpec((1, 1, D), lambda i: (row_of(i // nper), 0, 0))
    tabspec = pl.BlockSpec((tm, LANE), lambda i: (i % nper, 0))
    tile = lambda w: pl.BlockSpec((tm, w), lambda i: (i, 0))
    if with_q:
        kern = functools.partial(_front_odd_kernel, with_q=True)
        out_specs = [tile(DIFF_W), tile(DIFF_W), tile(DIFF_W), tile(FNET_W)]
        out_shape = [jax.ShapeDtypeStruct((n, DIFF_W), BF16)] * 3 + [jax.ShapeDtypeStruct((n, FNET_W), BF16)]
    else:
        def kern(x_ref, sh_ref, sc_ref, g0_ref, win_ref, cos_ref, sin_ref, k_ref, v_ref):
            _front_odd_kernel(x_ref, sh_ref, sc_ref, g0_ref, win_ref, cos_ref, sin_ref, None, k_ref, v_ref, None,
                              with_q=False)
        out_specs = [tile(DIFF_W), tile(DIFF_W)]
        out_shape = [jax.ShapeDtypeStruct((n, DIFF_W), BF16)] * 2
    return pl.pallas_call(
        kern,
        grid=(n // tm,),
        in_specs=[tile(D), rowspec, rowspec, _full((1, D)), _full((D, w_in.shape[1])), tabspec, tabspec],
        out_specs=out_specs,
        out_shape=out_shape,
        compiler_params=_cp(("parallel",)),
        name="front_odd" if with_q else "front_odd_ctx",
    )(x2d, sh, sc, g0, w_in, cos, sin)


def _diff_attn_kernel(q_ref, kc_ref, vc_ref, kl_ref, vl_ref, lam_ref, sg_ref, o_ref, *, lam_init, bk):
    tq = q_ref.shape[0]
    lp = lam_ref[...]
    lam = (jnp.exp(jnp.sum(lp[0:1, :] * lp[1:2, :], axis=-1, keepdims=True))
           - jnp.exp(jnp.sum(lp[2:3, :] * lp[3:4, :], axis=-1, keepdims=True)) + lam_init)
    lane = lax.broadcasted_iota(I32, (tq, LANE), 1)
    for hd in range(q_ref.shape[1] // LANE):
        sl = pl.ds(hd * LANE, LANE)
        q = q_ref[:, sl]
        zero = jnp.zeros_like(q)
        qq = jnp.concatenate([jnp.where(lane < DIFF_HD, q, zero), jnp.where(lane >= DIFF_HD, q, zero)], axis=0)
        blocks = (_key_blocks(kc_ref.at[:, sl], vc_ref.at[:, sl], bk)
                  + _key_blocks(kl_ref.at[:, sl], vl_ref.at[:, sl], bk))
        acc, l = _online_softmax_pv(qq, blocks)
        r = 1.0 / l
        o = acc[:tq] * r[:tq] - acc[tq:] * (r[tq:] * lam)
        o_ref[:, sl] = (_rms(o, sg_ref[...]) * (1.0 - lam_init)).astype(BF16)


def _diff_attn(q, k_ctx, v_ctx, k_lat, v_lat, lam_p, subln_g, nb, seq, ctx_len, lam_init):
    tq = min(TQ_DIFF, seq)
    nq = seq // tq
    return pl.pallas_call(
        functools.partial(_diff_attn_kernel, lam_init=lam_init, bk=BK_ATTN),
        grid=(nb, DIFF_H // HP_DIFF, nq),
        in_specs=[pl.BlockSpec((tq, HP_DIFF * LANE), lambda b, h, i: (b * nq + i, h)),
                  pl.BlockSpec((ctx_len, HP_DIFF * LANE), lambda b, h, i: (b, h)),
                  pl.BlockSpec((ctx_len, HP_DIFF * LANE), lambda b, h, i: (b, h)),
                  pl.BlockSpec((seq, HP_DIFF * LANE), lambda b, h, i: (b, h)),
                  pl.BlockSpec((seq, HP_DIFF * LANE), lambda b, h, i: (b, h)),
                  _full((4, DIFF_HD)), _full((1, 2 * DIFF_HD))],
        out_specs=pl.BlockSpec((tq, HP_DIFF * LANE), lambda b, h, i: (b * nq + i, h)),
        out_shape=jax.ShapeDtypeStruct((nb * seq, DIFF_W), BF16),
        compiler_params=_cp(("parallel", "parallel", "arbitrary")),
        name="diff_attn",
    )(q, k_ctx, v_ctx, k_lat, v_lat, lam_p, subln_g)


def _fourier_kernel(ct_ref, st_ref, f_ref, cc_ref, sc_ref, o_ref):
    f = f_ref[...]
    p = _dot(ct_ref[...], f).astype(BF16)
    q = _dot(st_ref[...], f).astype(BF16)
    o_ref[...] = (_dot(p, cc_ref[...]) - _dot(q, sc_ref[...])).astype(BF16)


def _fourier(f2d, ct, st, cc, sc, nb, seq):
    tk = min(TK_FFT, seq)
    nk = seq // tk
    return pl.pallas_call(
        _fourier_kernel,
        grid=(nk, nb),
        in_specs=[pl.BlockSpec((tk, seq), lambda j, b: (j, 0)), pl.BlockSpec((tk, seq), lambda j, b: (j, 0)),
                  pl.BlockSpec((seq, FNET_W), lambda j, b: (b, 0)), _full((FNET_W, FNET_W)), _full((FNET_W, FNET_W))],
        out_specs=pl.BlockSpec((tk, FNET_W), lambda j, b: (b * nk + j, 0)),
        out_shape=jax.ShapeDtypeStruct((nb * seq, FNET_W), BF16),
        compiler_params=_cp(("arbitrary", "arbitrary")),
        name="fourier",
    )(ct, st, f2d, cc, sc)


def _swiglu_chunks(x, wg_ref, wu_ref, wd_ref, n_chunks, between):
    f = wg_ref.shape[2]
    step = -(-f // (n_chunks * 256)) * 256
    acc = None
    for c in range(n_chunks):
        lo, hi = c * step, min((c + 1) * step, f)
        g = _dot(x, wg_ref[0, :, lo:hi])
        u = _dot(x, wu_ref[0, :, lo:hi])
        a = (g / (1.0 + jnp.exp(-g)) * u).astype(BF16)
        part = _dot(a, wd_ref[0, lo:hi, :])
        acc = part if acc is None else acc + part
        between(c)
    return acc


def _moe_expert_kernel(te_ref, tv_ref, idx_ref, idxn_ref, idxp_ref, h_ref, wg_ref, wu_ref, wd_ref, out_ref,
                       xbuf, ybuf, gsem, ssem, *, n_tok, n_chunks):
    tm = xbuf.shape[1]
    i = pl.program_id(0)
    n = pl.num_programs(0)
    cur = i % 2
    nvalid = tv_ref[i]
    nprev = jnp.where(i >= 1, tv_ref[jnp.maximum(i - 1, 0)], 0)
    next_used = jnp.logical_and(i + 1 < n, tv_ref[jnp.minimum(i + 1, n - 1)] > 0)

    def gather_row(ids_ref, b, r):
        j = ids_ref[0, 0, r]
        tok = jnp.where(j >= n_tok, j - n_tok, j)
        return pltpu.make_async_copy(h_ref.at[pl.ds(tok, 1)], xbuf.at[b, pl.ds(r, 1)], gsem.at[b])

    def scatter_row(ids_ref, b, r):
        return pltpu.make_async_copy(ybuf.at[b, pl.ds(r, 1)], out_ref.at[pl.ds(ids_ref[0, 0, r], 1)], ssem.at[b])

    def scatter_loop(ids_ref, b, nv):
        def issue(r, c):
            @pl.when(r < nv)
            def _():
                scatter_row(ids_ref, b, r).start()
            return c
        lax.fori_loop(0, tm, issue, 0)

    def wait_scatter(k):
        nv = tv_ref[k]
        b = k % 2

        @pl.when(nv == tm)
        def _():
            pltpu.make_async_copy(ybuf.at[b], out_ref.at[pl.ds(0, tm)], ssem.at[b]).wait()

        @pl.when(jnp.logical_and(nv > 0, nv < tm))
        def _():
            def drain(r, c):
                @pl.when(r < nv)
                def _():
                    pltpu.make_async_copy(ybuf.at[b, pl.ds(r, 1)], out_ref.at[pl.ds(0, 1)], ssem.at[b]).wait()
                return c
            lax.fori_loop(0, tm, drain, 0, unroll=8)

    @pl.when(jnp.logical_and(i == 0, nvalid > 0))
    def _():
        def issue(r, c):
            gather_row(idx_ref, 0, r).start()
            return c
        lax.fori_loop(0, tm, issue, 0)

    @pl.when(i >= 2)
    def _():
        wait_scatter(i - 2)

    @pl.when(nvalid > 0)
    def _():
        pltpu.make_async_copy(h_ref.at[pl.ds(0, tm)], xbuf.at[cur], gsem.at[cur]).wait()
        x = _unpack_bf16_pairs(xbuf[cur]).astype(BF16)

        def between(c):
            groups = n_chunks - 1
            if c >= groups:
                return
            for r in range(c * tm // groups, (c + 1) * tm // groups):
                @pl.when(next_used)
                def _():
                    gather_row(idxn_ref, 1 - cur, r).start()

                @pl.when(r < nprev)
                def _():
                    scatter_row(idxp_ref, 1 - cur, r).start()

        ybuf[cur] = _pack_bf16_pairs(_swiglu_chunks(x, wg_ref, wu_ref, wd_ref, n_chunks, between))

    @pl.when(jnp.logical_and(nvalid == 0, nprev > 0))
    def _():
        scatter_loop(idxp_ref, 1 - cur, nprev)

    @pl.when(i == n - 1)
    def _():
        @pl.when(nvalid > 0)
        def _():
            scatter_loop(idx_ref, cur, nvalid)

        @pl.when(i >= 1)
        def _():
            wait_scatter(i - 1)
        wait_scatter(i)


def _moe_experts(tile_expert, tile_valid, ids, h2p, wg, wu, wd):
    n_tiles, _, tm = ids.shape
    n_tok = h2p.shape[0]
    once = lambda shape: pl.BlockSpec(shape, lambda i, te, tv: (te[i], 0, 0), pipeline_mode=pl.Buffered(1))
    grid_spec = pltpu.PrefetchScalarGridSpec(
        num_scalar_prefetch=2,
        grid=(n_tiles,),
        in_specs=[pl.BlockSpec((1, 1, tm), lambda i, te, tv: (i, 0, 0), memory_space=pltpu.SMEM),
                  pl.BlockSpec((1, 1, tm), lambda i, te, tv: (jnp.minimum(i + 1, n_tiles - 1), 0, 0),
                               memory_space=pltpu.SMEM),
                  pl.BlockSpec((1, 1, tm), lambda i, te, tv: (jnp.maximum(i - 1, 0), 0, 0), memory_space=pltpu.SMEM),
                  pl.BlockSpec(memory_space=pl.ANY),
                  once((1, D, D_FF)), once((1, D, D_FF)), once((1, D_FF, D))],
        out_specs=pl.BlockSpec(memory_space=pl.ANY),
        scratch_shapes=[pltpu.VMEM((2, tm, D // 2), U32), pltpu.VMEM((2, tm, D // 2), U32),
                        pltpu.SemaphoreType.DMA((2,)), pltpu.SemaphoreType.DMA((2,))],
    )
    return pl.pallas_call(
        functools.partial(_moe_expert_kernel, n_tok=n_tok, n_chunks=N_DMA_GROUPS),
        grid_spec=grid_spec,
        out_shape=jax.ShapeDtypeStruct((TOP_K * n_tok, D // 2), U32),
        compiler_params=_cp(("arbitrary",)),
        name="moe_experts",
    )(tile_expert, tile_valid, ids, ids, ids, h2p, wg, wu, wd)


def _moe_combine_kernel(y0_ref, y1_ref, rwt_ref, x_ref, g2_ref, n3_ref, o_ref):
    w = rwt_ref[...]
    f = _unpack_bf16_pairs(y0_ref[...]) * w[:, 0:1] + _unpack_bf16_pairs(y1_ref[...]) * w[:, 1:2]
    o_ref[...] = x_ref[...] + g2_ref[0] * _rms(f, n3_ref[...])


def _moe_combine(ys, rwt, x1, g2, n3, row_of, seq):
    n = x1.shape[0]
    ts = min(TM, seq)
    nper = seq // ts
    nsteps = n // ts
    rowspec = pl.BlockSpec((1, 1, D), lambda i: (row_of(i // nper), 0, 0))
    return pl.pallas_call(
        _moe_combine_kernel,
        grid=(nsteps,),
        in_specs=[pl.BlockSpec((ts, D // 2), lambda i: (i, 0)), pl.BlockSpec((ts, D // 2), lambda i: (i + nsteps, 0)),
                  pl.BlockSpec((ts, LANE), lambda i: (i, 0)), pl.BlockSpec((ts, D), lambda i: (i, 0)),
                  rowspec, _full((1, D))],
        out_specs=pl.BlockSpec((ts, D), lambda i: (i, 0)),
        out_shape=jax.ShapeDtypeStruct((n, D), F32),
        compiler_params=_cp(("arbitrary",)),
        name="moe_combine",
    )(ys, ys, rwt, x1, g2, n3)


def _route(ridx, n, tm):
    n_asg = TOP_K * n
    e_flat = jnp.concatenate([ridx[:, 0], ridx[:, 1]])
    counts = jnp.sum((e_flat[:, None] == jnp.arange(N_EXP, dtype=I32)[None, :]).astype(I32), axis=0)
    ptiles = (counts + tm - 1) // tm
    pad = ptiles * tm - counts
    fill_e = jnp.repeat(jnp.arange(N_EXP, dtype=I32), tm)
    fill_k = jnp.tile(jnp.arange(tm, dtype=I32), N_EXP)
    fill_key = jnp.where(fill_k < pad[fill_e], fill_e, N_EXP)
    keys = jnp.concatenate([e_flat, fill_key])
    n_ent = n_asg + N_EXP * tm
    shift = max(n_ent - 1, 1).bit_length()
    assert (N_EXP + 1) << shift < 2 ** 31
    order = jnp.sort((keys << shift) | jnp.arange(n_ent, dtype=I32)) & ((1 << shift) - 1)
    ids = jnp.where(order < n_asg, order, 0)
    n_tiles = n_asg // tm + N_EXP
    tile_end = jnp.cumsum(ptiles)
    t = jnp.arange(n_tiles, dtype=I32)
    te = jnp.sum((t[:, None] >= tile_end[None, :]).astype(I32), axis=1)
    used = te < N_EXP
    last_e = jnp.max(jnp.where(counts > 0, jnp.arange(N_EXP, dtype=I32), 0))
    te_c = jnp.where(used, te, last_e).astype(I32)
    start = jnp.sum(jnp.where(t[:, None] >= tile_end[None, :], ptiles[None, :], 0), axis=1)
    left = counts[jnp.minimum(te, N_EXP - 1)] - (t - start) * tm
    tv = jnp.where(used, jnp.clip(left, 0, tm), 0).astype(I32)
    return ids.reshape(n_tiles, 1, tm), te_c, tv


def _rope_tables(seq):
    rows = seq // GRID_W
    row = np.repeat(np.arange(rows, dtype=np.float64), GRID_W)
    col = np.tile(np.arange(GRID_W, dtype=np.float64), rows)
    n_freq = MLA_ROPE // 4
    inv = ROPE_BASE ** (-np.arange(n_freq, dtype=np.float64) / n_freq)
    ang = np.concatenate([row[:, None] * inv, col[:, None] * inv], axis=-1)
    cos, sin = np.cos(ang), np.sin(ang)
    cos128 = np.tile(cos, (1, 4)).astype(np.float32)
    sin128 = np.tile(sin, (1, 4)).astype(np.float32)
    sin_signed = np.tile(np.concatenate([-sin, sin], axis=-1), (1, 2)).astype(np.float32)
    return jnp.asarray(cos128), jnp.asarray(sin128), jnp.asarray(sin_signed)


def _dft_tables(seq):
    k = np.arange(seq, dtype=np.int64)
    ang = 2.0 * np.pi * ((k[:, None] * k[None, :]) % seq).astype(np.float64) / seq
    ct = (np.cos(ang) / np.sqrt(seq)).astype(np.float32)
    st = (np.sin(ang) / np.sqrt(seq)).astype(np.float32)
    c = np.arange(FNET_GD, dtype=np.int64)
    angc = 2.0 * np.pi * ((c[:, None] * c[None, :]) % FNET_GD).astype(np.float64) / FNET_GD
    eye = np.eye(FNET_G)
    cc = np.kron(eye, np.cos(angc) / np.sqrt(FNET_GD)).astype(np.float32)
    sc = np.kron(eye, np.sin(angc) / np.sqrt(FNET_GD)).astype(np.float32)
    return (jnp.asarray(ct, dtype=BF16), jnp.asarray(st, dtype=BF16), jnp.asarray(cc, dtype=BF16),
            jnp.asarray(sc, dtype=BF16))


def _rot_half_cols(w):
    return jnp.concatenate([-w[..., MLA_ROPE // 2:], w[..., :MLA_ROPE // 2]], axis=-1)


def _even_weights(w_in, w_uq, w_ukv):
    o = 3 * SC_W + MLA_QR + MLA_KVR
    kpe = w_in[:, o:o + MLA_ROPE]
    z64 = jnp.zeros((D, LANE - MLA_ROPE), w_in.dtype)
    w_in_ext = jnp.concatenate([w_in[:, :o], kpe, z64, _rot_half_cols(kpe), z64], axis=1).astype(BF16)
    wq = w_uq.reshape(MLA_QR, MLA_H, MLA_NOPE + MLA_ROPE)
    zq = jnp.zeros((MLA_QR, MLA_H, LANE - MLA_ROPE), w_uq.dtype)
    wq_main = jnp.concatenate([wq, zq], axis=-1).reshape(MLA_QR, MLA_H * 256).astype(BF16)
    wq_swap = jnp.concatenate([_rot_half_cols(wq[..., MLA_NOPE:]), zq], axis=-1).reshape(MLA_QR, MLA_H * LANE)
    return w_in_ext, wq_main, wq_swap.astype(BF16), w_ukv.astype(BF16)


def kernel(x, c, ctx, c_ctx, ev_mod_w, ev_mod_b, ev_norm_g, ev_w_in, ev_conv_w, ev_q_norm_g, ev_w_uq, ev_kv_norm_g,
           ev_w_ukv, ev_w_out, ev_ffn_gate, ev_ffn_up, ev_ffn_down, od_mod_w, od_mod_b, od_norm_g, od_w_in,
           od_lambda, od_subln_g, od_w_out, od_router, od_exp_gate, od_exp_up, od_exp_down):
    nb, seq, _ = x.shape
    ctx_len = ctx.shape[1]
    n = nb * seq
    nc = nb * ctx_len
    assert seq % GRID_W == 0 and seq % 128 == 0 and ctx_len % 128 == 0
    mod_rows = ((nb + 1 + 7) // 8) * 8
    cond = jnp.zeros((mod_rows, D), F32).at[:nb].set(c).at[nb].set(c_ctx)
    lat_row = lambda b: b
    ctx_row = lambda b: nb

    cos128, sin128, sin_signed = _rope_tables(seq)
    ones_c = jnp.ones((ctx_len, LANE), F32)
    zeros_c = jnp.zeros((ctx_len, LANE), F32)
    x2d = x.reshape(n, D)
    c2d = ctx.reshape(nc, D)

    mods = _modulation(cond, ev_mod_w[0].astype(BF16), ev_mod_b[0])
    sh1, sc1, g1, sh2, sc2, g2 = [m.reshape(mod_rows, 1, D) for m in jnp.split(mods, N_MOD, axis=-1)]
    ng = ev_norm_g[0].reshape(4, 1, D)
    w_in_e, wq_main, wq_swap, wkv = _even_weights(ev_w_in[0], ev_w_uq[0], ev_w_ukv[0])
    qg = ev_q_norm_g[0].reshape(1, MLA_QR)
    kvg = ev_kv_norm_g[0].reshape(1, MLA_KVR)
    w_out_e = ev_w_out[0].astype(BF16)
    wg_e, wu_e, wd_e = ev_ffn_gate[0].astype(BF16), ev_ffn_up[0].astype(BF16), ev_ffn_down[0].astype(BF16)

    bg_l, uc_l, q_l, k_l, v_l = _front_even(x2d, sh1, sc1, lat_row, ng[0], w_in_e, qg, wq_main, wq_swap, kvg, wkv,
                                            cos128, sin128, seq)
    bg_c, uc_c, q_c, k_c, v_c = _front_even(c2d, sh1, sc1, ctx_row, ng[0], w_in_e, qg, wq_main, wq_swap, kvg, wkv,
                                            ones_c, zeros_c, ctx_len)
    at_l = _mla_attn(q_l, k_c, v_c, k_l, v_l, nb, seq, ctx_len)
    at_c = _mla_attn(q_c, k_c, v_c, None, None, nb, ctx_len, ctx_len)
    x1_l, h2_l = _mix_even(bg_l, uc_l, ev_conv_w[0], at_l, w_out_e, x2d, g1, ng[1], ng[2], sh2, sc2, lat_row, seq)
    x1_c, h2_c = _mix_even(bg_c, uc_c, ev_conv_w[0], at_c, w_out_e, c2d, g1, ng[1], ng[2], sh2, sc2, ctx_row,
                           ctx_len)
    x2d = _ffn_dense(h2_l, wg_e, wu_e, wd_e, x1_l, g2, ng[3], lat_row, seq)
    c2d = _ffn_dense(h2_c, wg_e, wu_e, wd_e, x1_c, g2, ng[3], ctx_row, ctx_len)

    lam_init = 0.8 - 0.6 * math.exp(-0.3 * 1)
    mods = _modulation(cond, od_mod_w[0].astype(BF16), od_mod_b[0])
    sh1, sc1, g1, sh2, sc2, g2 = [m.reshape(mod_rows, 1, D) for m in jnp.split(mods, N_MOD, axis=-1)]
    ng = od_norm_g[0].reshape(4, 1, D)
    w_in_o = od_w_in[0].astype(BF16)
    q_o, k_o, v_o, f_o = _front_odd(x2d, sh1, sc1, lat_row, ng[0], w_in_o, cos128, sin_signed, seq, True)
    kc_o, vc_o = _front_odd(c2d, sh1, sc1, ctx_row, ng[0], w_in_o[:, DIFF_W:3 * DIFF_W], ones_c, zeros_c, ctx_len,
                            False)
    ca = _diff_attn(q_o, kc_o, vc_o, k_o, v_o, od_lambda[0], od_subln_g[0].reshape(1, 2 * DIFF_HD), nb, seq,
                    ctx_len, lam_init)
    ct, st, cc, sc = _dft_tables(seq)
    fd = _fourier(f_o, ct, st, cc, sc, nb, seq)
    router_f = jnp.zeros((D, LANE), F32).at[:, :N_EXP].set(od_router[0])
    router_hi = router_f.astype(BF16)
    router_pad = jnp.concatenate([router_hi, (router_f - router_hi.astype(F32)).astype(BF16)], axis=1)
    x1, h2p, ridx, rwt = _mix_odd(ca, fd, od_w_out[0].astype(BF16), x2d, g1, ng[1], ng[2], sh2, sc2, router_pad,
                                  lat_row, seq)
    ids, tile_expert, tile_valid = _route(ridx, n, TM_EXP)
    ys = _moe_experts(tile_expert, tile_valid, ids, h2p, od_exp_gate[0].astype(BF16), od_exp_up[0].astype(BF16),
                      od_exp_down[0].astype(BF16))
    out = _moe_combine(ys, rwt, x1, g2, ng[3], lat_row, seq)
    return out.reshape(nb, seq, D)
```

```python
import functools
import math

import numpy as np
import jax
import jax.numpy as jnp
from jax import lax
from jax.experimental import pallas as pl
from jax.experimental.pallas import tpu as pltpu

F32 = jnp.float32
BF16 = jnp.bfloat16
I32 = jnp.int32
U32 = jnp.uint32

D = 1024
GRID_W = 64
EPS = 1e-6
ROPE_BASE = 10000.0
N_MOD = 6
SC_W = D // 2
MLA_V = 128
MLA_NOPE = 128
MLA_ROPE = 64
MLA_H = (D - SC_W) // MLA_V
MLA_QR = 3 * D // 8
MLA_KVR = D // 4
MLA_SCALE = (MLA_NOPE + MLA_ROPE) ** -0.5
DIFF_W = 3 * D // 4
DIFF_HD = 64
DIFF_H = DIFF_W // (2 * DIFF_HD)
DIFF_SCALE = DIFF_HD ** -0.5
FNET_W = D - DIFF_W
FNET_G = 4
FNET_GD = FNET_W // FNET_G
D_FF = ((8 * D // 3 + 127) // 128) * 128
N_EXP = 8
TOP_K = 2
LOG2E = math.log2(math.e)

LANE = 128
VMEM_LIMIT = 56 * 1024 * 1024

TM = 1024
TM_FFN = 512
N_ROW_SPLIT = 2
TQ_MLA = 1024
TQ_DIFF = 256
BK_ATTN = 256
HP_DIFF = 3
TK_FFT = 512
TM_EXP = 512
N_DMA_GROUPS = 4


def _cp(sem, vmem=VMEM_LIMIT):
    return pltpu.CompilerParams(dimension_semantics=sem, vmem_limit_bytes=vmem)


def _rms(x, g):
    return x * lax.rsqrt(jnp.mean(x * x, axis=-1, keepdims=True) + EPS) * g


def _dot(a, b):
    return jnp.dot(a, b, preferred_element_type=F32)


def _dot_nt(a, b):
    return lax.dot_general(a, b, (((1,), (1,)), ((), ())), preferred_element_type=F32)


def _full(shape):
    nd = len(shape)
    return pl.BlockSpec(shape, lambda *_: (0,) * nd)


def _pack_bf16_pairs(x):
    n = x.shape[1] // 2
    hi = pltpu.bitcast(x[:, :n].astype(BF16).astype(F32), U32)
    lo = pltpu.bitcast(x[:, n:].astype(BF16).astype(F32), U32)
    return hi | (lo >> 16)


def _unpack_bf16_pairs(u):
    hi = pltpu.bitcast(u & jnp.uint32(0xFFFF0000), F32)
    lo = pltpu.bitcast(u << 16, F32)
    return jnp.concatenate([hi, lo], axis=1)


def _mod_kernel(c_ref, w_ref, b_ref, o_ref):
    c = c_ref[...]
    s = c / (1.0 + jnp.exp(-c))
    o_ref[...] = _dot(s.astype(BF16), w_ref[...]) + b_ref[...]


def _modulation(cond, w_bf, b):
    rows = cond.shape[0]
    n = w_bf.shape[1]
    tn = 1536
    return pl.pallas_call(
        _mod_kernel,
        grid=(n // tn,),
        in_specs=[_full((rows, D)), pl.BlockSpec((D, tn), lambda j: (0, j)), pl.BlockSpec((1, tn), lambda j: (0, j))],
        out_specs=pl.BlockSpec((rows, tn), lambda j: (0, j)),
        out_shape=jax.ShapeDtypeStruct((rows, n), F32),
        compiler_params=_cp(("arbitrary",)),
        name="modulation",
    )(cond, w_bf, b.reshape(1, n))


def _front_even_kernel(x_ref, sh_ref, sc_ref, g0_ref, win_ref, qg_ref, wq_ref, wqs_ref, kvg_ref, wkv_ref,
                       cos_ref, sin_ref, bg_ref, uc_ref, q_ref, k_ref, v_ref):
    x = x_ref[...]
    h = _rms(x, g0_ref[...]) * (1.0 + sc_ref[0]) + sh_ref[0]
    z = _dot(h.astype(BF16), win_ref[...])
    bg_ref[...] = z[:, 0:SC_W].astype(BF16)
    uc_ref[...] = (z[:, SC_W:2 * SC_W] * z[:, 2 * SC_W:3 * SC_W]).astype(BF16)
    o = 3 * SC_W
    zq = z[:, o:o + MLA_QR]
    zkv = z[:, o + MLA_QR:o + MLA_QR + MLA_KVR]
    o2 = o + MLA_QR + MLA_KVR
    kpe = z[:, o2:o2 + LANE]
    kpes = z[:, o2 + LANE:o2 + 2 * LANE]
    cos = cos_ref[...]
    sin = sin_ref[...]
    zqn = _rms(zq, qg_ref[...]).astype(BF16)
    qm = _dot(zqn, wq_ref[...])
    qs = _dot(zqn, wqs_ref[...])
    qscale = MLA_SCALE * LOG2E
    for hd in range(MLA_H):
        lo = qm[:, 256 * hd:256 * hd + LANE]
        hi = qm[:, 256 * hd + LANE:256 * hd + 2 * LANE] * cos + qs[:, LANE * hd:LANE * hd + LANE] * sin
        q_ref[hd, :, 0:LANE] = (lo * qscale).astype(BF16)
        q_ref[hd, :, LANE:2 * LANE] = (hi * qscale).astype(BF16)
    zkvn = _rms(zkv, kvg_ref[...]).astype(BF16)
    kv = _dot(zkvn, wkv_ref[...])
    kpr = (kpe * cos + kpes * sin).astype(BF16)
    for hd in range(MLA_H):
        k_ref[hd, :, 0:LANE] = kv[:, 256 * hd:256 * hd + LANE].astype(BF16)
        k_ref[hd, :, LANE:2 * LANE] = kpr
        v_ref[hd] = kv[:, 256 * hd + LANE:256 * hd + 2 * LANE].astype(BF16)


def _front_even(x2d, sh, sc, row_of, g0, w_in, qg, wq, wqs, kvg, wkv, cos, sin, seq):
    n = x2d.shape[0]
    tm = min(TM, seq)
    nper = seq // tm
    rowspec = pl.BlockSpec((1, 1, D), lambda i: (row_of(i // nper), 0, 0))
    tabspec = pl.BlockSpec((tm, LANE), lambda i: (i % nper, 0))
    win_n = w_in.shape[1]
    return pl.pallas_call(
        _front_even_kernel,
        grid=(n // tm,),
        in_specs=[pl.BlockSpec((tm, D), lambda i: (i, 0)), rowspec, rowspec, _full((1, D)), _full((D, win_n)),
                  _full((1, MLA_QR)), _full((MLA_QR, 4 * 256)), _full((MLA_QR, 4 * LANE)),
                  _full((1, MLA_KVR)), _full((MLA_KVR, 4 * 256)), tabspec, tabspec],
        out_specs=[pl.BlockSpec((tm, SC_W), lambda i: (i, 0)), pl.BlockSpec((tm, SC_W), lambda i: (i, 0)),
                   pl.BlockSpec((MLA_H, tm, 256), lambda i: (0, i, 0)),
                   pl.BlockSpec((MLA_H, tm, 256), lambda i: (0, i, 0)),
                   pl.BlockSpec((MLA_H, tm, LANE), lambda i: (0, i, 0))],
        out_shape=[jax.ShapeDtypeStruct((n, SC_W), BF16), jax.ShapeDtypeStruct((n, SC_W), BF16),
                   jax.ShapeDtypeStruct((MLA_H, n, 256), BF16), jax.ShapeDtypeStruct((MLA_H, n, 256), BF16),
                   jax.ShapeDtypeStruct((MLA_H, n, LANE), BF16)],
        compiler_params=_cp(("parallel",)),
        name="front_even",
    )(x2d, sh, sc, g0, w_in, qg, wq, wqs, kvg, wkv, cos, sin)


def _key_blocks(k_ref, v_ref, bk):
    n = k_ref.shape[0]
    return [(k_ref.at[pl.ds(j, min(bk, n - j))], v_ref.at[pl.ds(j, min(bk, n - j))]) for j in range(0, n, bk)]


def _online_softmax_pv(q, blocks):
    m = acc = None
    dv = blocks[0][1].shape[1]
    for k_blk, v_blk in blocks:
        bk = k_blk.shape[0]
        ones_col = (lax.broadcasted_iota(I32, (bk, LANE), 1) == 0).astype(BF16)
        v_ext = jnp.concatenate([v_blk[...], ones_col], axis=1)
        s = _dot_nt(q, k_blk[...])
        mb = jnp.max(s, axis=-1, keepdims=True)
        if m is None:
            m = mb
            acc = _dot(jnp.exp2(s - m).astype(BF16), v_ext)
        else:
            m_new = jnp.maximum(m, mb)
            acc = jnp.exp2(m - m_new) * acc + _dot(jnp.exp2(s - m_new).astype(BF16), v_ext)
            m = m_new
    return acc[:, :dv], acc[:, dv:dv + 1]


def _mla_attn_kernel(*refs, with_lat, bk):
    if with_lat:
        q_ref, kc_ref, vc_ref, kl_ref, vl_ref, o_ref = refs
    else:
        q_ref, kc_ref, vc_ref, o_ref = refs
    blocks = _key_blocks(kc_ref.at[0], vc_ref.at[0], bk)
    if with_lat:
        blocks += _key_blocks(kl_ref.at[0], vl_ref.at[0], bk)
    acc, l = _online_softmax_pv(q_ref[0], blocks)
    o_ref[...] = (acc * (1.0 / l)).astype(BF16)


def _mla_attn(q, k_ctx, v_ctx, k_lat, v_lat, nb, seq_q, ctx_len):
    with_lat = k_lat is not None
    tq = min(TQ_MLA, seq_q)
    nq = seq_q // tq
    in_specs = [pl.BlockSpec((1, tq, 256), lambda b, h, i: (h, b * nq + i, 0)),
                pl.BlockSpec((1, ctx_len, 256), lambda b, h, i: (h, b, 0)),
                pl.BlockSpec((1, ctx_len, LANE), lambda b, h, i: (h, b, 0))]
    args = [q, k_ctx, v_ctx]
    if with_lat:
        in_specs += [pl.BlockSpec((1, seq_q, 256), lambda b, h, i: (h, b, 0)),
                     pl.BlockSpec((1, seq_q, LANE), lambda b, h, i: (h, b, 0))]
        args += [k_lat, v_lat]
    return pl.pallas_call(
        functools.partial(_mla_attn_kernel, with_lat=with_lat, bk=BK_ATTN),
        grid=(nb, MLA_H, nq),
        in_specs=in_specs,
        out_specs=pl.BlockSpec((tq, LANE), lambda b, h, i: (b * nq + i, h)),
        out_shape=jax.ShapeDtypeStruct((nb * seq_q, MLA_H * MLA_V), BF16),
        compiler_params=_cp(("parallel", "parallel", "arbitrary")),
        name="mla_attn_lat" if with_lat else "mla_attn_ctx",
    )(*args)


def _residual_and_h2(y, x, g1_ref, n1_ref, n2_ref, sh2_ref, sc2_ref):
    x1 = x + g1_ref[0] * _rms(y, n1_ref[...])
    h2 = _rms(x1, n2_ref[...]) * (1.0 + sc2_ref[0]) + sh2_ref[0]
    return x1, h2


def _mix_even_kernel(bg_ref, uc_ref, ucp_ref, ucn_ref, cw_ref, at_ref, wo_ref, x_ref, g1_ref, n1_ref, n2_ref,
                     sh2_ref, sc2_ref, x1_ref, h2_ref, scr, *, nper):
    tm = uc_ref.shape[0]
    i = pl.program_id(0)
    ucf = uc_ref[...].astype(F32)
    first = (i % nper) == 0
    last = (i % nper) == nper - 1
    prev_row = jnp.where(first, 0.0, ucp_ref[7:8, :].astype(F32))
    next_row = jnp.where(last, 0.0, ucn_ref[0:1, :].astype(F32))
    scr[8:8 + tm, :] = ucf
    scr[7:8, :] = prev_row
    scr[8 + tm:9 + tm, :] = next_row
    up = scr[7:7 + tm, :]
    dn = scr[9:9 + tm, :]
    conv = cw_ref[0:1, :] * up + cw_ref[1:2, :] * ucf + cw_ref[2:3, :] * dn
    a = (bg_ref[...].astype(F32) * conv).astype(BF16)
    rows = tm // N_ROW_SPLIT
    for p in range(N_ROW_SPLIT):
        rs = slice(p * rows, (p + 1) * rows)
        y = _dot(a[rs], wo_ref[0:SC_W, :]) + _dot(at_ref[rs, :], wo_ref[SC_W:D, :])
        x1, h2 = _residual_and_h2(y, x_ref[rs, :], g1_ref, n1_ref, n2_ref, sh2_ref, sc2_ref)
        x1_ref[rs, :] = x1
        h2_ref[rs, :] = h2.astype(BF16)


def _mix_even(bg, uc, conv_w, attn, w_out, x2d, g1, n1, n2, sh2, sc2, row_of, seq):
    n = x2d.shape[0]
    tm = min(TM, seq)
    nper = seq // tm
    nb8 = n // 8
    rowspec = pl.BlockSpec((1, 1, D), lambda i: (row_of(i // nper), 0, 0))
    tile = lambda w: pl.BlockSpec((tm, w), lambda i: (i, 0))
    return pl.pallas_call(
        functools.partial(_mix_even_kernel, nper=nper),
        grid=(n // tm,),
        in_specs=[tile(SC_W), tile(SC_W),
                  pl.BlockSpec((8, SC_W), lambda i: (jnp.maximum(i * (tm // 8) - 1, 0), 0)),
                  pl.BlockSpec((8, SC_W), lambda i: (jnp.minimum((i + 1) * (tm // 8), nb8 - 1), 0)),
                  _full((3, SC_W)), tile(MLA_H * MLA_V), _full((D, D)), tile(D), rowspec, _full((1, D)),
                  _full((1, D)), rowspec, rowspec],
        out_specs=[tile(D), tile(D)],
        out_shape=[jax.ShapeDtypeStruct((n, D), F32), jax.ShapeDtypeStruct((n, D), BF16)],
        scratch_shapes=[pltpu.VMEM((tm + 16, SC_W), F32)],
        compiler_params=_cp(("parallel",)),
        name="mix_even",
    )(bg, uc, uc, uc, conv_w, attn, w_out, x2d, g1, n1, n2, sh2, sc2)


def _mix_odd_kernel(ca_ref, fd_ref, wo_ref, x_ref, g1_ref, n1_ref, n2_ref, sh2_ref, sc2_ref, rw_ref,
                    x1_ref, h2p_ref, ridx_ref, rwt_ref):
    y = _dot(ca_ref[...], wo_ref[0:DIFF_W, :]) + _dot(fd_ref[...], wo_ref[DIFF_W:D, :])
    x1, h2 = _residual_and_h2(y, x_ref[...], g1_ref, n1_ref, n2_ref, sh2_ref, sc2_ref)
    x1_ref[...] = x1
    h2p_ref[...] = _pack_bf16_pairs(h2)
    tm = h2.shape[0]
    hi = h2.astype(BF16)
    lo = (h2 - hi.astype(F32)).astype(BF16)
    r = _dot(jnp.concatenate([hi, lo], axis=0), rw_ref[...])
    logits = (r[:tm, :LANE] + r[:tm, LANE:]) + (r[tm:, :LANE] + r[tm:, LANE:])
    lane = lax.broadcasted_iota(I32, logits.shape, 1).astype(F32)
    neg = jnp.float32(-jnp.inf)
    s0 = jnp.where(lane < N_EXP, logits, neg)
    m1 = jnp.max(s0, axis=-1, keepdims=True)
    i1 = jnp.min(jnp.where(s0 == m1, lane, float(LANE)), axis=-1, keepdims=True)
    s1 = jnp.where(lane == i1, neg, s0)
    m2 = jnp.max(s1, axis=-1, keepdims=True)
    i2 = jnp.min(jnp.where(s1 == m2, lane, float(LANE)), axis=-1, keepdims=True)
    e = jnp.exp(m2 - m1)
    w1 = 1.0 / (1.0 + e)
    w2 = e * w1
    ridx_ref[...] = jnp.where(lane == 0.0, i1, jnp.where(lane == 1.0, i2, 0.0)).astype(I32)
    rwt_ref[...] = jnp.where(lane == 0.0, w1, jnp.where(lane == 1.0, w2, 0.0))


def _mix_odd(cattn, fd, w_out, x2d, g1, n1, n2, sh2, sc2, router_pad, row_of, seq):
    n = x2d.shape[0]
    tm = min(TM, seq)
    nper = seq // tm
    rowspec = pl.BlockSpec((1, 1, D), lambda i: (row_of(i // nper), 0, 0))
    tile = lambda w: pl.BlockSpec((tm, w), lambda i: (i, 0))
    return pl.pallas_call(
        _mix_odd_kernel,
        grid=(n // tm,),
        in_specs=[tile(DIFF_W), tile(FNET_W), _full((D, D)), tile(D), rowspec, _full((1, D)), _full((1, D)),
                  rowspec, rowspec, _full((D, 2 * LANE))],
        out_specs=[tile(D), tile(D // 2), tile(LANE), tile(LANE)],
        out_shape=[jax.ShapeDtypeStruct((n, D), F32), jax.ShapeDtypeStruct((n, D // 2), U32),
                   jax.ShapeDtypeStruct((n, LANE), I32), jax.ShapeDtypeStruct((n, LANE), F32)],
        compiler_params=_cp(("parallel",)),
        name="mix_odd",
    )(cattn, fd, w_out, x2d, g1, n1, n2, sh2, sc2, router_pad)


def _swiglu(h, wg, wu, wd):
    g = _dot(h, wg)
    u = _dot(h, wu)
    a = (g / (1.0 + jnp.exp(-g)) * u).astype(BF16)
    return _dot(a, wd)


def _ffn_kernel(h_ref, wg_ref, wu_ref, wd_ref, x_ref, g2_ref, n3_ref, o_ref):
    f = _swiglu(h_ref[...], wg_ref[...], wu_ref[...], wd_ref[...])
    o_ref[...] = x_ref[...] + g2_ref[0] * _rms(f, n3_ref[...])


def _ffn_dense(h2, wg, wu, wd, x1, g2, n3, row_of, seq):
    n = x1.shape[0]
    tm = min(TM_FFN, seq)
    nper = seq // tm
    rowspec = pl.BlockSpec((1, 1, D), lambda i: (row_of(i // nper), 0, 0))
    tile = lambda w: pl.BlockSpec((tm, w), lambda i: (i, 0))
    once = lambda shape: pl.BlockSpec(shape, lambda i: (0, 0), pipeline_mode=pl.Buffered(1))
    return pl.pallas_call(
        _ffn_kernel,
        grid=(n // tm,),
        in_specs=[tile(D), once((D, D_FF)), once((D, D_FF)), once((D_FF, D)), tile(D), rowspec, _full((1, D))],
        out_specs=tile(D),
        out_shape=jax.ShapeDtypeStruct((n, D), F32),
        compiler_params=_cp(("parallel",)),
        name="ffn_dense",
    )(h2, wg, wu, wd, x1, g2, n3)


def _rope_slab(x, cos, sin_signed, lane):
    swap = jnp.where((lane & 63) < 32, pltpu.roll(x, 96, 1), pltpu.roll(x, 32, 1))
    return x * cos + swap * sin_signed


def _front_odd_kernel(x_ref, sh_ref, sc_ref, g0_ref, win_ref, cos_ref, sin_ref, q_ref, k_ref, v_ref, f_ref,
                      *, with_q):
    x = x_ref[...]
    h = _rms(x, g0_ref[...]) * (1.0 + sc_ref[0]) + sh_ref[0]
    z = _dot(h.astype(BF16), win_ref[...])
    cos = cos_ref[...]
    sin = sin_ref[...]
    lane = lax.broadcasted_iota(I32, cos.shape, 1)
    off = DIFF_W if with_q else 0
    qscale = DIFF_SCALE * LOG2E
    for g in range(DIFF_W // LANE):
        sl = slice(LANE * g, LANE * g + LANE)
        if with_q:
            q_ref[:, sl] = (_rope_slab(z[:, sl], cos, sin, lane) * qscale).astype(BF16)
        ksl = slice(off + LANE * g, off + LANE * g + LANE)
        k_ref[:, sl] = _rope_slab(z[:, ksl], cos, sin, lane).astype(BF16)
    v_ref[...] = z[:, off + DIFF_W:off + 2 * DIFF_W].astype(BF16)
    if with_q:
        f_ref[...] = z[:, 3 * DIFF_W:3 * DIFF_W + FNET_W].astype(BF16)


def _front_odd(x2d, sh, sc, row_of, g0, w_in, cos, sin, seq, with_q):
    n = x2d.shape[0]
    tm = min(TM, seq)
    nper = seq // tm
    rowspec = pl.BlockSpec((1, 1, D), lambda i: (row_of(i // nper), 0, 0))
    tabspec = pl.BlockSpec((tm, LANE), lambda i: (i % nper, 0))
    tile = lambda w: pl.BlockSpec((tm, w), lambda i: (i, 0))
    if with_q:
        kern = functools.partial(_front_odd_kernel, with_q=True)
        out_specs = [tile(DIFF_W), tile(DIFF_W), tile(DIFF_W), tile(FNET_W)]
        out_shape = [jax.ShapeDtypeStruct((n, DIFF_W), BF16)] * 3 + [jax.ShapeDtypeStruct((n, FNET_W), BF16)]
    else:
        def kern(x_ref, sh_ref, sc_ref, g0_ref, win_ref, cos_ref, sin_ref, k_ref, v_ref):
            _front_odd_kernel(x_ref, sh_ref, sc_ref, g0_ref, win_ref, cos_ref, sin_ref, None, k_ref, v_ref, None,
                              with_q=False)
        out_specs = [tile(DIFF_W), tile(DIFF_W)]
        out_shape = [jax.ShapeDtypeStruct((n, DIFF_W), BF16)] * 2
    return pl.pallas_call(
        kern,
        grid=(n // tm,),
        in_specs=[tile(D), rowspec, rowspec, _full((1, D)), _full((D, w_in.shape[1])), tabspec, tabspec],
        out_specs=out_specs,
        out_shape=out_shape,
        compiler_params=_cp(("parallel",)),
        name="front_odd" if with_q else "front_odd_ctx",
    )(x2d, sh, sc, g0, w_in, cos, sin)


def _diff_attn_kernel(q_ref, kc_ref, vc_ref, kl_ref, vl_ref, lam_ref, sg_ref, o_ref, *, lam_init, bk):
    tq = q_ref.shape[0]
    lp = lam_ref[...]
    lam = (jnp.exp(jnp.sum(lp[0:1, :] * lp[1:2, :], axis=-1, keepdims=True))
           - jnp.exp(jnp.sum(lp[2:3, :] * lp[3:4, :], axis=-1, keepdims=True)) + lam_init)
    lane = lax.broadcasted_iota(I32, (tq, LANE), 1)
    for hd in range(q_ref.shape[1] // LANE):
        sl = pl.ds(hd * LANE, LANE)
        q = q_ref[:, sl]
        zero = jnp.zeros_like(q)
        qq = jnp.concatenate([jnp.where(lane < DIFF_HD, q, zero), jnp.where(lane >= DIFF_HD, q, zero)], axis=0)
        blocks = (_key_blocks(kc_ref.at[:, sl], vc_ref.at[:, sl], bk)
                  + _key_blocks(kl_ref.at[:, sl], vl_ref.at[:, sl], bk))
        acc, l = _online_softmax_pv(qq, blocks)
        r = 1.0 / l
        o = acc[:tq] * r[:tq] - acc[tq:] * (r[tq:] * lam)
        o_ref[:, sl] = (_rms(o, sg_ref[...]) * (1.0 - lam_init)).astype(BF16)


def _diff_attn(q, k_ctx, v_ctx, k_lat, v_lat, lam_p, subln_g, nb, seq, ctx_len, lam_init):
    tq = min(TQ_DIFF, seq)
    nq = seq // tq
    return pl.pallas_call(
        functools.partial(_diff_attn_kernel, lam_init=lam_init, bk=BK_ATTN),
        grid=(nb, DIFF_H // HP_DIFF, nq),
        in_specs=[pl.BlockSpec((tq, HP_DIFF * LANE), lambda b, h, i: (b * nq + i, h)),
                  pl.BlockSpec((ctx_len, HP_DIFF * LANE), lambda b, h, i: (b, h)),
                  pl.BlockSpec((ctx_len, HP_DIFF * LANE), lambda b, h, i: (b, h)),
                  pl.BlockSpec((seq, HP_DIFF * LANE), lambda b, h, i: (b, h)),
                  pl.BlockSpec((seq, HP_DIFF * LANE), lambda b, h, i: (b, h)),
                  _full((4, DIFF_HD)), _full((1, 2 * DIFF_HD))],
        out_specs=pl.BlockSpec((tq, HP_DIFF * LANE), lambda b, h, i: (b * nq + i, h)),
        out_shape=jax.ShapeDtypeStruct((nb * seq, DIFF_W), BF16),
        compiler_params=_cp(("parallel", "parallel", "arbitrary")),
        name="diff_attn",
    )(q, k_ctx, v_ctx, k_lat, v_lat, lam_p, subln_g)


def _fourier_kernel(ct_ref, st_ref, f_ref, cc_ref, sc_ref, o_ref):
    f = f_ref[...]
    p = _dot(ct_ref[...], f).astype(BF16)
    q = _dot(st_ref[...], f).astype(BF16)
    o_ref[...] = (_dot(p, cc_ref[...]) - _dot(q, sc_ref[...])).astype(BF16)


def _fourier(f2d, ct, st, cc, sc, nb, seq):
    tk = min(TK_FFT, seq)
    nk = seq // tk
    return pl.pallas_call(
        _fourier_kernel,
        grid=(nk, nb),
        in_specs=[pl.BlockSpec((tk, seq), lambda j, b: (j, 0)), pl.BlockSpec((tk, seq), lambda j, b: (j, 0)),
                  pl.BlockSpec((seq, FNET_W), lambda j, b: (b, 0)), _full((FNET_W, FNET_W)), _full((FNET_W, FNET_W))],
        out_specs=pl.BlockSpec((tk, FNET_W), lambda j, b: (b * nk + j, 0)),
        out_shape=jax.ShapeDtypeStruct((nb * seq, FNET_W), BF16),
        compiler_params=_cp(("arbitrary", "arbitrary")),
        name="fourier",
    )(ct, st, f2d, cc, sc)


def _swiglu_chunks(x, wg_ref, wu_ref, wd_ref, n_chunks, between):
    f = wg_ref.shape[2]
    step = -(-f // (n_chunks * 256)) * 256
    acc = None
    for c in range(n_chunks):
        lo, hi = c * step, min((c + 1) * step, f)
        g = _dot(x, wg_ref[0, :, lo:hi])
        u = _dot(x, wu_ref[0, :, lo:hi])
        a = (g / (1.0 + jnp.exp(-g)) * u).astype(BF16)
        part = _dot(a, wd_ref[0, lo:hi, :])
        acc = part if acc is None else acc + part
        between(c)
    return acc


def _moe_expert_kernel(te_ref, tv_ref, idx_ref, idxn_ref, idxp_ref, h_ref, wg_ref, wu_ref, wd_ref, out_ref,
                       xbuf, ybuf, gsem, ssem, *, n_tok, n_chunks):
    tm = xbuf.shape[1]
    i = pl.program_id(0)
    n = pl.num_programs(0)
    cur = i % 2
    nvalid = tv_ref[i]
    nprev = jnp.where(i >= 1, tv_ref[jnp.maximum(i - 1, 0)], 0)
    next_used = jnp.logical_and(i + 1 < n, tv_ref[jnp.minimum(i + 1, n - 1)] > 0)

    def gather_row(ids_ref, b, r):
        j = ids_ref[0, 0, r]
        tok = jnp.where(j >= n_tok, j - n_tok, j)
        return pltpu.make_async_copy(h_ref.at[pl.ds(tok, 1)], xbuf.at[b, pl.ds(r, 1)], gsem.at[b])

    def scatter_row(ids_ref, b, r):
        return pltpu.make_async_copy(ybuf.at[b, pl.ds(r, 1)], out_ref.at[pl.ds(ids_ref[0, 0, r], 1)], ssem.at[b])

    def scatter_loop(ids_ref, b, nv):
        def issue(r, c):
            @pl.when(r < nv)
            def _():
                scatter_row(ids_ref, b, r).start()
            return c
        lax.fori_loop(0, tm, issue, 0)

    def wait_scatter(k):
        nv = tv_ref[k]
        b = k % 2

        @pl.when(nv == tm)
        def _():
            pltpu.make_async_copy(ybuf.at[b], out_ref.at[pl.ds(0, tm)], ssem.at[b]).wait()

        @pl.when(jnp.logical_and(nv > 0, nv < tm))
        def _():
            def drain(r, c):
                @pl.when(r < nv)
                def _():
                    pltpu.make_async_copy(ybuf.at[b, pl.ds(r, 1)], out_ref.at[pl.ds(0, 1)], ssem.at[b]).wait()
                return c
            lax.fori_loop(0, tm, drain, 0, unroll=8)

    @pl.when(jnp.logical_and(i == 0, nvalid > 0))
    def _():
        def issue(r, c):
            gather_row(idx_ref, 0, r).start()
            return c
        lax.fori_loop(0, tm, issue, 0)

    @pl.when(i >= 2)
    def _():
        wait_scatter(i - 2)

    @pl.when(nvalid > 0)
    def _():
        pltpu.make_async_copy(h_ref.at[pl.ds(0, tm)], xbuf.at[cur], gsem.at[cur]).wait()
        x = _unpack_bf16_pairs(xbuf[cur]).astype(BF16)

        def between(c):
            half = tm // 2
            if c < 2:
                for r in range(c * half, (c + 1) * half):
                    @pl.when(next_used)
                    def _():
                        gather_row(idxn_ref, 1 - cur, r).start(priority=0)
            if 1 <= c < 3:
                for r in range((c - 1) * half, c * half):
                    @pl.when(r < nprev)
                    def _():
                        scatter_row(idxp_ref, 1 - cur, r).start(priority=1)

        ybuf[cur] = _pack_bf16_pairs(_swiglu_chunks(x, wg_ref, wu_ref, wd_ref, n_chunks, between))

    @pl.when(jnp.logical_and(nvalid == 0, nprev > 0))
    def _():
        scatter_loop(idxp_ref, 1 - cur, nprev)

    @pl.when(i == n - 1)
    def _():
        @pl.when(nvalid > 0)
        def _():
            scatter_loop(idx_ref, cur, nvalid)

        @pl.when(i >= 1)
        def _():
            wait_scatter(i - 1)
        wait_scatter(i)


def _moe_experts(tile_expert, tile_valid, ids, h2p, wg, wu, wd):
    n_tiles, _, tm = ids.shape
    n_tok = h2p.shape[0]
    once = lambda shape: pl.BlockSpec(shape, lambda i, te, tv: (te[i], 0, 0), pipeline_mode=pl.Buffered(1))
    grid_spec = pltpu.PrefetchScalarGridSpec(
        num_scalar_prefetch=2,
        grid=(n_tiles,),
        in_specs=[pl.BlockSpec((1, 1, tm), lambda i, te, tv: (i, 0, 0), memory_space=pltpu.SMEM),
                  pl.BlockSpec((1, 1, tm), lambda i, te, tv: (jnp.minimum(i + 1, n_tiles - 1), 0, 0),
                               memory_space=pltpu.SMEM),
                  pl.BlockSpec((1, 1, tm), lambda i, te, tv: (jnp.maximum(i - 1, 0), 0, 0), memory_space=pltpu.SMEM),
                  pl.BlockSpec(memory_space=pl.ANY),
                  once((1, D, D_FF)), once((1, D, D_FF)), once((1, D_FF, D))],
        out_specs=pl.BlockSpec(memory_space=pl.ANY),
        scratch_shapes=[pltpu.VMEM((2, tm, D // 2), U32), pltpu.VMEM((2, tm, D // 2), U32),
                        pltpu.SemaphoreType.DMA((2,)), pltpu.SemaphoreType.DMA((2,))],
    )
    return pl.pallas_call(
        functools.partial(_moe_expert_kernel, n_tok=n_tok, n_chunks=N_DMA_GROUPS),
        grid_spec=grid_spec,
        out_shape=jax.ShapeDtypeStruct((TOP_K * n_tok, D // 2), U32),
        compiler_params=_cp(("arbitrary",)),
        name="moe_experts",
    )(tile_expert, tile_valid, ids, ids, ids, h2p, wg, wu, wd)


def _moe_combine_kernel(y0_ref, y1_ref, rwt_ref, x_ref, g2_ref, n3_ref, o_ref):
    w = rwt_ref[...]
    f = _unpack_bf16_pairs(y0_ref[...]) * w[:, 0:1] + _unpack_bf16_pairs(y1_ref[...]) * w[:, 1:2]
    o_ref[...] = x_ref[...] + g2_ref[0] * _rms(f, n3_ref[...])


def _moe_combine(ys, rwt, x1, g2, n3, row_of, seq):
    n = x1.shape[0]
    ts = min(TM_FFN, seq)
    nper = seq // ts
    nsteps = n // ts
    rowspec = pl.BlockSpec((1, 1, D), lambda i: (row_of(i // nper), 0, 0))
    return pl.pallas_call(
        _moe_combine_kernel,
        grid=(nsteps,),
        in_specs=[pl.BlockSpec((ts, D // 2), lambda i: (i, 0)), pl.BlockSpec((ts, D // 2), lambda i: (i + nsteps, 0)),
                  pl.BlockSpec((ts, LANE), lambda i: (i, 0)), pl.BlockSpec((ts, D), lambda i: (i, 0)),
                  rowspec, _full((1, D))],
        out_specs=pl.BlockSpec((ts, D), lambda i: (i, 0)),
        out_shape=jax.ShapeDtypeStruct((n, D), F32),
        compiler_params=_cp(("arbitrary",)),
        name="moe_combine",
    )(ys, ys, rwt, x1, g2, n3)


def _route(ridx, n, tm):
    n_asg = TOP_K * n
    e_flat = jnp.concatenate([ridx[:, 0], ridx[:, 1]])
    counts = jnp.sum((e_flat[:, None] == jnp.arange(N_EXP, dtype=I32)[None, :]).astype(I32), axis=0)
    ptiles = (counts + tm - 1) // tm
    pad = ptiles * tm - counts
    fill_e = jnp.repeat(jnp.arange(N_EXP, dtype=I32), tm)
    fill_k = jnp.tile(jnp.arange(tm, dtype=I32), N_EXP)
    fill_key = jnp.where(fill_k < pad[fill_e], fill_e, N_EXP)
    keys = jnp.concatenate([e_flat, fill_key])
    n_ent = n_asg + N_EXP * tm
    shift = max(n_ent - 1, 1).bit_length()
    assert (N_EXP + 1) << shift < 2 ** 31
    order = jnp.sort((keys << shift) | jnp.arange(n_ent, dtype=I32)) & ((1 << shift) - 1)
    ids = jnp.where(order < n_asg, order, 0)
    n_tiles = n_asg // tm + N_EXP
    tile_end = jnp.cumsum(ptiles)
    t = jnp.arange(n_tiles, dtype=I32)
    te = jnp.sum((t[:, None] >= tile_end[None, :]).astype(I32), axis=1)
    used = te < N_EXP
    last_e = jnp.max(jnp.where(counts > 0, jnp.arange(N_EXP, dtype=I32), 0))
    te_c = jnp.where(used, te, last_e).astype(I32)
    start = jnp.sum(jnp.where(t[:, None] >= tile_end[None, :], ptiles[None, :], 0), axis=1)
    left = counts[jnp.minimum(te, N_EXP - 1)] - (t - start) * tm
    tv = jnp.where(used, jnp.clip(left, 0, tm), 0).astype(I32)
    return ids.reshape(n_tiles, 1, tm), te_c, tv


def _rope_tables(seq):
    rows = seq // GRID_W
    row = np.repeat(np.arange(rows, dtype=np.float64), GRID_W)
    col = np.tile(np.arange(GRID_W, dtype=np.float64), rows)
    n_freq = MLA_ROPE // 4
    inv = ROPE_BASE ** (-np.arange(n_freq, dtype=np.float64) / n_freq)
    ang = np.concatenate([row[:, None] * inv, col[:, None] * inv], axis=-1)
    cos, sin = np.cos(ang), np.sin(ang)
    cos128 = np.tile(cos, (1, 4)).astype(np.float32)
    sin128 = np.tile(sin, (1, 4)).astype(np.float32)
    sin_signed = np.tile(np.concatenate([-sin, sin], axis=-1), (1, 2)).astype(np.float32)
    return jnp.asarray(cos128), jnp.asarray(sin128), jnp.asarray(sin_signed)


def _dft_tables(seq):
    k = np.arange(seq, dtype=np.int64)
    ang = 2.0 * np.pi * ((k[:, None] * k[None, :]) % seq).astype(np.float64) / seq
    ct = (np.cos(ang) / np.sqrt(seq)).astype(np.float32)
    st = (np.sin(ang) / np.sqrt(seq)).astype(np.float32)
    c = np.arange(FNET_GD, dtype=np.int64)
    angc = 2.0 * np.pi * ((c[:, None] * c[None, :]) % FNET_GD).astype(np.float64) / FNET_GD
    eye = np.eye(FNET_G)
    cc = np.kron(eye, np.cos(angc) / np.sqrt(FNET_GD)).astype(np.float32)
    sc = np.kron(eye, np.sin(angc) / np.sqrt(FNET_GD)).astype(np.float32)
    return (jnp.asarray(ct, dtype=BF16), jnp.asarray(st, dtype=BF16), jnp.asarray(cc, dtype=BF16),
            jnp.asarray(sc, dtype=BF16))


def _rot_half_cols(w):
    return jnp.concatenate([-w[..., MLA_ROPE // 2:], w[..., :MLA_ROPE // 2]], axis=-1)


def _even_weights(w_in, w_uq, w_ukv):
    o = 3 * SC_W + MLA_QR + MLA_KVR
    kpe = w_in[:, o:o + MLA_ROPE]
    z64 = jnp.zeros((D, LANE - MLA_ROPE), w_in.dtype)
    w_in_ext = jnp.concatenate([w_in[:, :o], kpe, z64, _rot_half_cols(kpe), z64], axis=1).astype(BF16)
    wq = w_uq.reshape(MLA_QR, MLA_H, MLA_NOPE + MLA_ROPE)
    zq = jnp.zeros((MLA_QR, MLA_H, LANE - MLA_ROPE), w_uq.dtype)
    wq_main = jnp.concatenate([wq, zq], axis=-1).reshape(MLA_QR, MLA_H * 256).astype(BF16)
    wq_swap = jnp.concatenate([_rot_half_cols(wq[..., MLA_NOPE:]), zq], axis=-1).reshape(MLA_QR, MLA_H * LANE)
    return w_in_ext, wq_main, wq_swap.astype(BF16), w_ukv.astype(BF16)


def kernel(x, c, ctx, c_ctx, ev_mod_w, ev_mod_b, ev_norm_g, ev_w_in, ev_conv_w, ev_q_norm_g, ev_w_uq, ev_kv_norm_g,
           ev_w_ukv, ev_w_out, ev_ffn_gate, ev_ffn_up, ev_ffn_down, od_mod_w, od_mod_b, od_norm_g, od_w_in,
           od_lambda, od_subln_g, od_w_out, od_router, od_exp_gate, od_exp_up, od_exp_down):
    nb, seq, _ = x.shape
    ctx_len = ctx.shape[1]
    n = nb * seq
    nc = nb * ctx_len
    assert seq % GRID_W == 0 and seq % 128 == 0 and ctx_len % 128 == 0
    mod_rows = ((nb + 1 + 7) // 8) * 8
    cond = jnp.zeros((mod_rows, D), F32).at[:nb].set(c).at[nb].set(c_ctx)
    lat_row = lambda b: b
    ctx_row = lambda b: nb

    cos128, sin128, sin_signed = _rope_tables(seq)
    ones_c = jnp.ones((ctx_len, LANE), F32)
    zeros_c = jnp.zeros((ctx_len, LANE), F32)
    x2d = x.reshape(n, D)
    c2d = ctx.reshape(nc, D)

    mods = _modulation(cond, ev_mod_w[0].astype(BF16), ev_mod_b[0])
    sh1, sc1, g1, sh2, sc2, g2 = [m.reshape(mod_rows, 1, D) for m in jnp.split(mods, N_MOD, axis=-1)]
    ng = ev_norm_g[0].reshape(4, 1, D)
    w_in_e, wq_main, wq_swap, wkv = _even_weights(ev_w_in[0], ev_w_uq[0], ev_w_ukv[0])
    qg = ev_q_norm_g[0].reshape(1, MLA_QR)
    kvg = ev_kv_norm_g[0].reshape(1, MLA_KVR)
    w_out_e = ev_w_out[0].astype(BF16)
    wg_e, wu_e, wd_e = ev_ffn_gate[0].astype(BF16), ev_ffn_up[0].astype(BF16), ev_ffn_down[0].astype(BF16)

    bg_l, uc_l, q_l, k_l, v_l = _front_even(x2d, sh1, sc1, lat_row, ng[0], w_in_e, qg, wq_main, wq_swap, kvg, wkv,
                                            cos128, sin128, seq)
    bg_c, uc_c, q_c, k_c, v_c = _front_even(c2d, sh1, sc1, ctx_row, ng[0], w_in_e, qg, wq_main, wq_swap, kvg, wkv,
                                            ones_c, zeros_c, ctx_len)
    at_l = _mla_attn(q_l, k_c, v_c, k_l, v_l, nb, seq, ctx_len)
    at_c = _mla_attn(q_c, k_c, v_c, None, None, nb, ctx_len, ctx_len)
    x1_l, h2_l = _mix_even(bg_l, uc_l, ev_conv_w[0], at_l, w_out_e, x2d, g1, ng[1], ng[2], sh2, sc2, lat_row, seq)
    x1_c, h2_c = _mix_even(bg_c, uc_c, ev_conv_w[0], at_c, w_out_e, c2d, g1, ng[1], ng[2], sh2, sc2, ctx_row,
                           ctx_len)
    x2d = _ffn_dense(h2_l, wg_e, wu_e, wd_e, x1_l, g2, ng[3], lat_row, seq)
    c2d = _ffn_dense(h2_c, wg_e, wu_e, wd_e, x1_c, g2, ng[3], ctx_row, ctx_len)

    lam_init = 0.8 - 0.6 * math.exp(-0.3 * 1)
    mods = _modulation(cond, od_mod_w[0].astype(BF16), od_mod_b[0])
    sh1, sc1, g1, sh2, sc2, g2 = [m.reshape(mod_rows, 1, D) for m in jnp.split(mods, N_MOD, axis=-1)]
    ng = od_norm_g[0].reshape(4, 1, D)
    w_in_o = od_w_in[0].astype(BF16)
    q_o, k_o, v_o, f_o = _front_odd(x2d, sh1, sc1, lat_row, ng[0], w_in_o, cos128, sin_signed, seq, True)
    kc_o, vc_o = _front_odd(c2d, sh1, sc1, ctx_row, ng[0], w_in_o[:, DIFF_W:3 * DIFF_W], ones_c, zeros_c, ctx_len,
                            False)
    ca = _diff_attn(q_o, kc_o, vc_o, k_o, v_o, od_lambda[0], od_subln_g[0].reshape(1, 2 * DIFF_HD), nb, seq,
                    ctx_len, lam_init)
    ct, st, cc, sc = _dft_tables(seq)
    fd = _fourier(f_o, ct, st, cc, sc, nb, seq)
    router_f = jnp.zeros((D, LANE), F32).at[:, :N_EXP].set(od_router[0])
    router_hi = router_f.astype(BF16)
    router_pad = jnp.concatenate([router_hi, (router_f - router_hi.astype(F32)).astype(BF16)], axis=1)
    x1, h2p, ridx, rwt = _mix_odd(ca, fd, od_w_out[0].astype(BF16), x2d, g1, ng[1], ng[2], sh2, sc2, router_pad,
                                  lat_row, seq)
    ids, tile_expert, tile_valid = _route(ridx, n, TM_EXP)
    ys = _moe_experts(tile_expert, tile_valid, ids, h2p, od_exp_gate[0].astype(BF16), od_exp_up[0].astype(BF16),
                      od_exp_down[0].astype(BF16))
    out = _moe_combine(ys, rwt, x1, g2, ng[3], lat_row, seq)
    return out.reshape(nb, seq, D)
```

```python
import functools
import math

import numpy as np
import jax
import jax.numpy as jnp
from jax import lax
from jax.experimental import pallas as pl
from jax.experimental.pallas import tpu as pltpu

F32 = jnp.float32
BF16 = jnp.bfloat16
I32 = jnp.int32
U32 = jnp.uint32

D = 1024
GRID_W = 64
EPS = 1e-6
ROPE_BASE = 10000.0
N_MOD = 6
SC_W = D // 2
MLA_V = 128
MLA_NOPE = 128
MLA_ROPE = 64
MLA_H = (D - SC_W) // MLA_V
MLA_QR = 3 * D // 8
MLA_KVR = D // 4
MLA_SCALE = (MLA_NOPE + MLA_ROPE) ** -0.5
DIFF_W = 3 * D // 4
DIFF_HD = 64
DIFF_H = DIFF_W // (2 * DIFF_HD)
DIFF_SCALE = DIFF_HD ** -0.5
FNET_W = D - DIFF_W
FNET_G = 4
FNET_GD = FNET_W // FNET_G
D_FF = ((8 * D // 3 + 127) // 128) * 128
N_EXP = 8
TOP_K = 2
LOG2E = math.log2(math.e)

LANE = 128
VMEM_LIMIT = 56 * 1024 * 1024

TM = 1024
TM_FFN = 512
N_ROW_SPLIT = 2
TQ_MLA = 1024
TQ_DIFF = 256
BK_ATTN = 256
HP_DIFF = 3
TK_FFT = 512
TM_EXP = 512
N_DMA_GROUPS = 2


def _cp(sem, vmem=VMEM_LIMIT):
    return pltpu.CompilerParams(dimension_semantics=sem, vmem_limit_bytes=vmem)


def _rms(x, g):
    return x * lax.rsqrt(jnp.mean(x * x, axis=-1, keepdims=True) + EPS) * g


def _dot(a, b):
    return jnp.dot(a, b, preferred_element_type=F32)


def _dot_nt(a, b):
    return lax.dot_general(a, b, (((1,), (1,)), ((), ())), preferred_element_type=F32)


def _full(shape):
    nd = len(shape)
    return pl.BlockSpec(shape, lambda *_: (0,) * nd)


def _pack_bf16_pairs(x):
    n = x.shape[1] // 2
    hi = pltpu.bitcast(x[:, :n].astype(BF16).astype(F32), U32)
    lo = pltpu.bitcast(x[:, n:].astype(BF16).astype(F32), U32)
    return hi | (lo >> 16)


def _unpack_bf16_pairs(u):
    hi = pltpu.bitcast(u & jnp.uint32(0xFFFF0000), F32)
    lo = pltpu.bitcast(u << 16, F32)
    return jnp.concatenate([hi, lo], axis=1)


def _mod_kernel(c_ref, w_ref, b_ref, o_ref):
    c = c_ref[...]
    s = c / (1.0 + jnp.exp(-c))
    o_ref[...] = _dot(s.astype(BF16), w_ref[...]) + b_ref[...]


def _modulation(cond, w_bf, b):
    rows = cond.shape[0]
    n = w_bf.shape[1]
    tn = 1536
    return pl.pallas_call(
        _mod_kernel,
        grid=(n // tn,),
        in_specs=[_full((rows, D)), pl.BlockSpec((D, tn), lambda j: (0, j)), pl.BlockSpec((1, tn), lambda j: (0, j))],
        out_specs=pl.BlockSpec((rows, tn), lambda j: (0, j)),
        out_shape=jax.ShapeDtypeStruct((rows, n), F32),
        compiler_params=_cp(("arbitrary",)),
        name="modulation",
    )(cond, w_bf, b.reshape(1, n))


def _front_even_kernel(x_ref, sh_ref, sc_ref, g0_ref, win_ref, qg_ref, wq_ref, wqs_ref, kvg_ref, wkv_ref,
                       cos_ref, sin_ref, bg_ref, uc_ref, q_ref, k_ref, v_ref):
    x = x_ref[...]
    h = _rms(x, g0_ref[...]) * (1.0 + sc_ref[0]) + sh_ref[0]
    z = _dot(h.astype(BF16), win_ref[...])
    bg_ref[...] = z[:, 0:SC_W].astype(BF16)
    uc_ref[...] = (z[:, SC_W:2 * SC_W] * z[:, 2 * SC_W:3 * SC_W]).astype(BF16)
    o = 3 * SC_W
    zq = z[:, o:o + MLA_QR]
    zkv = z[:, o + MLA_QR:o + MLA_QR + MLA_KVR]
    o2 = o + MLA_QR + MLA_KVR
    kpe = z[:, o2:o2 + LANE]
    kpes = z[:, o2 + LANE:o2 + 2 * LANE]
    cos = cos_ref[...]
    sin = sin_ref[...]
    zqn = _rms(zq, qg_ref[...]).astype(BF16)
    qm = _dot(zqn, wq_ref[...])
    qs = _dot(zqn, wqs_ref[...])
    qscale = MLA_SCALE * LOG2E
    for hd in range(MLA_H):
        lo = qm[:, 256 * hd:256 * hd + LANE]
        hi = qm[:, 256 * hd + LANE:256 * hd + 2 * LANE] * cos + qs[:, LANE * hd:LANE * hd + LANE] * sin
        q_ref[hd, :, 0:LANE] = (lo * qscale).astype(BF16)
        q_ref[hd, :, LANE:2 * LANE] = (hi * qscale).astype(BF16)
    zkvn = _rms(zkv, kvg_ref[...]).astype(BF16)
    kv = _dot(zkvn, wkv_ref[...])
    kpr = (kpe * cos + kpes * sin).astype(BF16)
    for hd in range(MLA_H):
        k_ref[hd, :, 0:LANE] = kv[:, 256 * hd:256 * hd + LANE].astype(BF16)
        k_ref[hd, :, LANE:2 * LANE] = kpr
        v_ref[hd] = kv[:, 256 * hd + LANE:256 * hd + 2 * LANE].astype(BF16)


def _front_even(x2d, sh, sc, row_of, g0, w_in, qg, wq, wqs, kvg, wkv, cos, sin, seq):
    n = x2d.shape[0]
    tm = min(TM, seq)
    nper = seq // tm
    rowspec = pl.BlockSpec((1, 1, D), lambda i: (row_of(i // nper), 0, 0))
    tabspec = pl.BlockSpec((tm, LANE), lambda i: (i % nper, 0))
    win_n = w_in.shape[1]
    return pl.pallas_call(
        _front_even_kernel,
        grid=(n // tm,),
        in_specs=[pl.BlockSpec((tm, D), lambda i: (i, 0)), rowspec, rowspec, _full((1, D)), _full((D, win_n)),
                  _full((1, MLA_QR)), _full((MLA_QR, 4 * 256)), _full((MLA_QR, 4 * LANE)),
                  _full((1, MLA_KVR)), _full((MLA_KVR, 4 * 256)), tabspec, tabspec],
        out_specs=[pl.BlockSpec((tm, SC_W), lambda i: (i, 0)), pl.BlockSpec((tm, SC_W), lambda i: (i, 0)),
                   pl.BlockSpec((MLA_H, tm, 256), lambda i: (0, i, 0)),
                   pl.BlockSpec((MLA_H, tm, 256), lambda i: (0, i, 0)),
                   pl.BlockSpec((MLA_H, tm, LANE), lambda i: (0, i, 0))],
        out_shape=[jax.ShapeDtypeStruct((n, SC_W), BF16), jax.ShapeDtypeStruct((n, SC_W), BF16),
                   jax.ShapeDtypeStruct((MLA_H, n, 256), BF16), jax.ShapeDtypeStruct((MLA_H, n, 256), BF16),
                   jax.ShapeDtypeStruct((MLA_H, n, LANE), BF16)],
        compiler_params=_cp(("parallel",)),
        name="front_even",
    )(x2d, sh, sc, g0, w_in, qg, wq, wqs, kvg, wkv, cos, sin)


def _key_blocks(k_ref, v_ref, bk):
    n = k_ref.shape[0]
    return [(k_ref.at[pl.ds(j, min(bk, n - j))], v_ref.at[pl.ds(j, min(bk, n - j))]) for j in range(0, n, bk)]


def _online_softmax_pv(q, blocks):
    m = acc = None
    dv = blocks[0][1].shape[1]
    for k_blk, v_blk in blocks:
        bk = k_blk.shape[0]
        ones_col = (lax.broadcasted_iota(I32, (bk, LANE), 1) == 0).astype(BF16)
        v_ext = jnp.concatenate([v_blk[...], ones_col], axis=1)
        s = _dot_nt(q, k_blk[...])
        mb = jnp.max(s, axis=-1, keepdims=True)
        if m is None:
            m = mb
            acc = _dot(jnp.exp2(s - m).astype(BF16), v_ext)
        else:
            m_new = jnp.maximum(m, mb)
            acc = jnp.exp2(m - m_new) * acc + _dot(jnp.exp2(s - m_new).astype(BF16), v_ext)
            m = m_new
    return acc[:, :dv], acc[:, dv:dv + 1]


def _mla_attn_kernel(*refs, with_lat, bk):
    if with_lat:
        q_ref, kc_ref, vc_ref, kl_ref, vl_ref, o_ref = refs
    else:
        q_ref, kc_ref, vc_ref, o_ref = refs
    blocks = _key_blocks(kc_ref.at[0], vc_ref.at[0], bk)
    if with_lat:
        blocks += _key_blocks(kl_ref.at[0], vl_ref.at[0], bk)
    acc, l = _online_softmax_pv(q_ref[0], blocks)
    o_ref[...] = (acc * (1.0 / l)).astype(BF16)


def _mla_attn(q, k_ctx, v_ctx, k_lat, v_lat, nb, seq_q, ctx_len):
    with_lat = k_lat is not None
    tq = min(TQ_MLA, seq_q)
    nq = seq_q // tq
    in_specs = [pl.BlockSpec((1, tq, 256), lambda b, h, i: (h, b * nq + i, 0)),
                pl.BlockSpec((1, ctx_len, 256), lambda b, h, i: (h, b, 0)),
                pl.BlockSpec((1, ctx_len, LANE), lambda b, h, i: (h, b, 0))]
    args = [q, k_ctx, v_ctx]
    if with_lat:
        in_specs += [pl.BlockSpec((1, seq_q, 256), lambda b, h, i: (h, b, 0)),
                     pl.BlockSpec((1, seq_q, LANE), lambda b, h, i: (h, b, 0))]
        args += [k_lat, v_lat]
    return pl.pallas_call(
        functools.partial(_mla_attn_kernel, with_lat=with_lat, bk=BK_ATTN),
        grid=(nb, MLA_H, nq),
        in_specs=in_specs,
        out_specs=pl.BlockSpec((tq, LANE), lambda b, h, i: (b * nq + i, h)),
        out_shape=jax.ShapeDtypeStruct((nb * seq_q, MLA_H * MLA_V), BF16),
        compiler_params=_cp(("parallel", "parallel", "arbitrary")),
        name="mla_attn_lat" if with_lat else "mla_attn_ctx",
    )(*args)


def _residual_and_h2(y, x, g1_ref, n1_ref, n2_ref, sh2_ref, sc2_ref):
    x1 = x + g1_ref[0] * _rms(y, n1_ref[...])
    h2 = _rms(x1, n2_ref[...]) * (1.0 + sc2_ref[0]) + sh2_ref[0]
    return x1, h2


def _mix_even_kernel(bg_ref, uc_ref, ucp_ref, ucn_ref, cw_ref, at_ref, wo_ref, x_ref, g1_ref, n1_ref, n2_ref,
                     sh2_ref, sc2_ref, x1_ref, h2_ref, scr, *, nper):
    tm = uc_ref.shape[0]
    i = pl.program_id(0)
    ucf = uc_ref[...].astype(F32)
    first = (i % nper) == 0
    last = (i % nper) == nper - 1
    prev_row = jnp.where(first, 0.0, ucp_ref[7:8, :].astype(F32))
    next_row = jnp.where(last, 0.0, ucn_ref[0:1, :].astype(F32))
    scr[8:8 + tm, :] = ucf
    scr[7:8, :] = prev_row
    scr[8 + tm:9 + tm, :] = next_row
    up = scr[7:7 + tm, :]
    dn = scr[9:9 + tm, :]
    conv = cw_ref[0:1, :] * up + cw_ref[1:2, :] * ucf + cw_ref[2:3, :] * dn
    a = (bg_ref[...].astype(F32) * conv).astype(BF16)
    rows = tm // N_ROW_SPLIT
    for p in range(N_ROW_SPLIT):
        rs = slice(p * rows, (p + 1) * rows)
        y = _dot(a[rs], wo_ref[0:SC_W, :]) + _dot(at_ref[rs, :], wo_ref[SC_W:D, :])
        x1, h2 = _residual_and_h2(y, x_ref[rs, :], g1_ref, n1_ref, n2_ref, sh2_ref, sc2_ref)
        x1_ref[rs, :] = x1
        h2_ref[rs, :] = h2.astype(BF16)


def _mix_even(bg, uc, conv_w, attn, w_out, x2d, g1, n1, n2, sh2, sc2, row_of, seq):
    n = x2d.shape[0]
    tm = min(TM, seq)
    nper = seq // tm
    nb8 = n // 8
    rowspec = pl.BlockSpec((1, 1, D), lambda i: (row_of(i // nper), 0, 0))
    tile = lambda w: pl.BlockSpec((tm, w), lambda i: (i, 0))
    return pl.pallas_call(
        functools.partial(_mix_even_kernel, nper=nper),
        grid=(n // tm,),
        in_specs=[tile(SC_W), tile(SC_W),
                  pl.BlockSpec((8, SC_W), lambda i: (jnp.maximum(i * (tm // 8) - 1, 0), 0)),
                  pl.BlockSpec((8, SC_W), lambda i: (jnp.minimum((i + 1) * (tm // 8), nb8 - 1), 0)),
                  _full((3, SC_W)), tile(MLA_H * MLA_V), _full((D, D)), tile(D), rowspec, _full((1, D)),
                  _full((1, D)), rowspec, rowspec],
        out_specs=[tile(D), tile(D)],
        out_shape=[jax.ShapeDtypeStruct((n, D), F32), jax.ShapeDtypeStruct((n, D), BF16)],
        scratch_shapes=[pltpu.VMEM((tm + 16, SC_W), F32)],
        compiler_params=_cp(("parallel",)),
        name="mix_even",
    )(bg, uc, uc, uc, conv_w, attn, w_out, x2d, g1, n1, n2, sh2, sc2)


def _mix_odd_kernel(ca_ref, fd_ref, wo_ref, x_ref, g1_ref, n1_ref, n2_ref, sh2_ref, sc2_ref, rw_ref,
                    x1_ref, h2p_ref, ridx_ref, rwt_ref):
    y = _dot(ca_ref[...], wo_ref[0:DIFF_W, :]) + _dot(fd_ref[...], wo_ref[DIFF_W:D, :])
    x1, h2 = _residual_and_h2(y, x_ref[...], g1_ref, n1_ref, n2_ref, sh2_ref, sc2_ref)
    x1_ref[...] = x1
    h2p_ref[...] = _pack_bf16_pairs(h2)
    tm = h2.shape[0]
    hi = h2.astype(BF16)
    lo = (h2 - hi.astype(F32)).astype(BF16)
    r = _dot(jnp.concatenate([hi, lo], axis=0), rw_ref[...])
    logits = (r[:tm, :LANE] + r[:tm, LANE:]) + (r[tm:, :LANE] + r[tm:, LANE:])
    lane = lax.broadcasted_iota(I32, logits.shape, 1).astype(F32)
    neg = jnp.float32(-jnp.inf)
    s0 = jnp.where(lane < N_EXP, logits, neg)
    m1 = jnp.max(s0, axis=-1, keepdims=True)
    i1 = jnp.min(jnp.where(s0 == m1, lane, float(LANE)), axis=-1, keepdims=True)
    s1 = jnp.where(lane == i1, neg, s0)
    m2 = jnp.max(s1, axis=-1, keepdims=True)
    i2 = jnp.min(jnp.where(s1 == m2, lane, float(LANE)), axis=-1, keepdims=True)
    e = jnp.exp(m2 - m1)
    w1 = 1.0 / (1.0 + e)
    w2 = e * w1
    ridx_ref[...] = jnp.where(lane == 0.0, i1, jnp.where(lane == 1.0, i2, 0.0)).astype(I32)
    rwt_ref[...] = jnp.where(lane == 0.0, w1, jnp.where(lane == 1.0, w2, 0.0))


def _mix_odd(cattn, fd, w_out, x2d, g1, n1, n2, sh2, sc2, router_pad, row_of, seq):
    n = x2d.shape[0]
    tm = min(TM, seq)
    nper = seq // tm
    rowspec = pl.BlockSpec((1, 1, D), lambda i: (row_of(i // nper), 0, 0))
    tile = lambda w: pl.BlockSpec((tm, w), lambda i: (i, 0))
    return pl.pallas_call(
        _mix_odd_kernel,
        grid=(n // tm,),
        in_specs=[tile(DIFF_W), tile(FNET_W), _full((D, D)), tile(D), rowspec, _full((1, D)), _full((1, D)),
                  rowspec, rowspec, _full((D, 2 * LANE))],
        out_specs=[tile(D), tile(D // 2), tile(LANE), tile(LANE)],
        out_shape=[jax.ShapeDtypeStruct((n, D), F32), jax.ShapeDtypeStruct((n, D // 2), U32),
                   jax.ShapeDtypeStruct((n, LANE), I32), jax.ShapeDtypeStruct((n, LANE), F32)],
        compiler_params=_cp(("parallel",)),
        name="mix_odd",
    )(cattn, fd, w_out, x2d, g1, n1, n2, sh2, sc2, router_pad)


def _swiglu(h, wg, wu, wd):
    g = _dot(h, wg)
    u = _dot(h, wu)
    a = (g / (1.0 + jnp.exp(-g)) * u).astype(BF16)
    return _dot(a, wd)


def _ffn_kernel(h_ref, wg_ref, wu_ref, wd_ref, x_ref, g2_ref, n3_ref, o_ref):
    f = _swiglu(h_ref[...], wg_ref[...], wu_ref[...], wd_ref[...])
    o_ref[...] = x_ref[...] + g2_ref[0] * _rms(f, n3_ref[...])


def _ffn_dense(h2, wg, wu, wd, x1, g2, n3, row_of, seq):
    n = x1.shape[0]
    tm = min(TM_FFN, seq)
    nper = seq // tm
    rowspec = pl.BlockSpec((1, 1, D), lambda i: (row_of(i // nper), 0, 0))
    tile = lambda w: pl.BlockSpec((tm, w), lambda i: (i, 0))
    once = lambda shape: pl.BlockSpec(shape, lambda i: (0, 0), pipeline_mode=pl.Buffered(1))
    return pl.pallas_call(
        _ffn_kernel,
        grid=(n // tm,),
        in_specs=[tile(D), once((D, D_FF)), once((D, D_FF)), once((D_FF, D)), tile(D), rowspec, _full((1, D))],
        out_specs=tile(D),
        out_shape=jax.ShapeDtypeStruct((n, D), F32),
        compiler_params=_cp(("parallel",)),
        name="ffn_dense",
    )(h2, wg, wu, wd, x1, g2, n3)


def _rope_slab(x, cos, sin_signed, lane):
    swap = jnp.where((lane & 63) < 32, pltpu.roll(x, 96, 1), pltpu.roll(x, 32, 1))
    return x * cos + swap * sin_signed


def _front_odd_kernel(x_ref, sh_ref, sc_ref, g0_ref, win_ref, cos_ref, sin_ref, q_ref, k_ref, v_ref, f_ref,
                      *, with_q):
    x = x_ref[...]
    h = _rms(x, g0_ref[...]) * (1.0 + sc_ref[0]) + sh_ref[0]
    z = _dot(h.astype(BF16), win_ref[...])
    cos = cos_ref[...]
    sin = sin_ref[...]
    lane = lax.broadcasted_iota(I32, cos.shape, 1)
    off = DIFF_W if with_q else 0
    qscale = DIFF_SCALE * LOG2E
    for g in range(DIFF_W // LANE):
        sl = slice(LANE * g, LANE * g + LANE)
        if with_q:
            q_ref[:, sl] = (_rope_slab(z[:, sl], cos, sin, lane) * qscale).astype(BF16)
        ksl = slice(off + LANE * g, off + LANE * g + LANE)
        k_ref[:, sl] = _rope_slab(z[:, ksl], cos, sin, lane).astype(BF16)
    v_ref[...] = z[:, off + DIFF_W:off + 2 * DIFF_W].astype(BF16)
    if with_q:
        f_ref[...] = z[:, 3 * DIFF_W:3 * DIFF_W + FNET_W].astype(BF16)


def _front_odd(x2d, sh, sc, row_of, g0, w_in, cos, sin, seq, with_q):
    n = x2d.shape[0]
    tm = min(TM, seq)
    nper = seq // tm
    rowspec = pl.BlockSpec((1, 1, D), lambda i: (row_of(i // nper), 0, 0))
    tabspec = pl.BlockSpec((tm, LANE), lambda i: (i % nper, 0))
    tile = lambda w: pl.BlockSpec((tm, w), lambda i: (i, 0))
    if with_q:
        kern = functools.partial(_front_odd_kernel, with_q=True)
        out_specs = [tile(DIFF_W), tile(DIFF_W), tile(DIFF_W), tile(FNET_W)]
        out_shape = [jax.ShapeDtypeStruct((n, DIFF_W), BF16)] * 3 + [jax.ShapeDtypeStruct((n, FNET_W), BF16)]
    else:
        def kern(x_ref, sh_ref, sc_ref, g0_ref, win_ref, cos_ref, sin_ref, k_ref, v_ref):
            _front_odd_kernel(x_ref, sh_ref, sc_ref, g0_ref, win_ref, cos_ref, sin_ref, None, k_ref, v_ref, None,
                              with_q=False)
        out_specs = [tile(DIFF_W), tile(DIFF_W)]
        out_shape = [jax.ShapeDtypeStruct((n, DIFF_W), BF16)] * 2
    return pl.pallas_call(
        kern,
        grid=(n // tm,),
        in_specs=[tile(D), rowspec, rowspec, _full((1, D)), _full((D, w_in.shape[1])), tabspec, tabspec],
        out_specs=out_specs,
        out_shape=out_shape,
        compiler_params=_cp(("parallel",)),
        name="front_odd" if with_q else "front_odd_ctx",
    )(x2d, sh, sc, g0, w_in, cos, sin)


def _diff_attn_kernel(q_ref, kc_ref, vc_ref, kl_ref, vl_ref, lam_ref, sg_ref, o_ref, *, lam_init, bk):
    tq = q_ref.shape[0]
    lp = lam_ref[...]
    lam = (jnp.exp(jnp.sum(lp[0:1, :] * lp[1:2, :], axis=-1, keepdims=True))
           - jnp.exp(jnp.sum(lp[2:3, :] * lp[3:4, :], axis=-1, keepdims=True)) + lam_init)
    lane = lax.broadcasted_iota(I32, (tq, LANE), 1)
    for hd in range(q_ref.shape[1] // LANE):
        sl = pl.ds(hd * LANE, LANE)
        q = q_ref[:, sl]
        zero = jnp.zeros_like(q)
        qq = jnp.concatenate([jnp.where(lane < DIFF_HD, q, zero), jnp.where(lane >= DIFF_HD, q, zero)], axis=0)
        blocks = (_key_blocks(kc_ref.at[:, sl], vc_ref.at[:, sl], bk)
                  + _key_blocks(kl_ref.at[:, sl], vl_ref.at[:, sl], bk))
        acc, l = _online_softmax_pv(qq, blocks)
        r = 1.0 / l
        o = acc[:tq] * r[:tq] - acc[tq:] * (r[tq:] * lam)
        o_ref[:, sl] = (_rms(o, sg_ref[...]) * (1.0 - lam_init)).astype(BF16)


def _diff_attn(q, k_ctx, v_ctx, k_lat, v_lat, lam_p, subln_g, nb, seq, ctx_len, lam_init):
    tq = min(TQ_DIFF, seq)
    nq = seq // tq
    return pl.pallas_call(
        functools.partial(_diff_attn_kernel, lam_init=lam_init, bk=BK_ATTN),
        grid=(nb, DIFF_H // HP_DIFF, nq),
        in_specs=[pl.BlockSpec((tq, HP_DIFF * LANE), lambda b, h, i: (b * nq + i, h)),
                  pl.BlockSpec((ctx_len, HP_DIFF * LANE), lambda b, h, i: (b, h)),
                  pl.BlockSpec((ctx_len, HP_DIFF * LANE), lambda b, h, i: (b, h)),
                  pl.BlockSpec((seq, HP_DIFF * LANE), lambda b, h, i: (b, h)),
                  pl.BlockSpec((seq, HP_DIFF * LANE), lambda b, h, i: (b, h)),
                  _full((4, DIFF_HD)), _full((1, 2 * DIFF_HD))],
        out_specs=pl.BlockSpec((tq, HP_DIFF * LANE), lambda b, h, i: (b * nq + i, h)),
        out_shape=jax.ShapeDtypeStruct((nb * seq, DIFF_W), BF16),
        compiler_params=_cp(("parallel", "parallel", "arbitrary")),
        name="diff_attn",
    )(q, k_ctx, v_ctx, k_lat, v_lat, lam_p, subln_g)


def _fourier_kernel(ct_ref, st_ref, f_ref, cc_ref, sc_ref, o_ref):
    f = f_ref[...]
    p = _dot(ct_ref[...], f).astype(BF16)
    q = _dot(st_ref[...], f).astype(BF16)
    o_ref[...] = (_dot(p, cc_ref[...]) - _dot(q, sc_ref[...])).astype(BF16)


def _fourier(f2d, ct, st, cc, sc, nb, seq):
    tk = min(TK_FFT, seq)
    nk = seq // tk
    return pl.pallas_call(
        _fourier_kernel,
        grid=(nk, nb),
        in_specs=[pl.BlockSpec((tk, seq), lambda j, b: (j, 0)), pl.BlockSpec((tk, seq), lambda j, b: (j, 0)),
                  pl.BlockSpec((seq, FNET_W), lambda j, b: (b, 0)), _full((FNET_W, FNET_W)), _full((FNET_W, FNET_W))],
        out_specs=pl.BlockSpec((tk, FNET_W), lambda j, b: (b * nk + j, 0)),
        out_shape=jax.ShapeDtypeStruct((nb * seq, FNET_W), BF16),
        compiler_params=_cp(("arbitrary", "arbitrary")),
        name="fourier",
    )(ct, st, f2d, cc, sc)


def _swiglu_chunks(x, wg_ref, wu_ref, wd_ref, n_chunks, between):
    f = wg_ref.shape[2]
    step = -(-f // (n_chunks * 256)) * 256
    acc = None
    for c in range(n_chunks):
        lo, hi = c * step, min((c + 1) * step, f)
        g = _dot(x, wg_ref[0, :, lo:hi])
        u = _dot(x, wu_ref[0, :, lo:hi])
        a = (g / (1.0 + jnp.exp(-g)) * u).astype(BF16)
        part = _dot(a, wd_ref[0, lo:hi, :])
        acc = part if acc is None else acc + part
        between(c)
    return acc


def _moe_expert_kernel(te_ref, tv_ref, idx_ref, idxn_ref, idxp_ref, h_ref, wg_ref, wu_ref, wd_ref, out_ref,
                       xbuf, ybuf, gsem, ssem, *, n_tok, n_chunks):
    tm = xbuf.shape[1]
    i = pl.program_id(0)
    n = pl.num_programs(0)
    cur = i % 2
    nvalid = tv_ref[i]
    nprev = jnp.where(i >= 1, tv_ref[jnp.maximum(i - 1, 0)], 0)
    next_used = jnp.logical_and(i + 1 < n, tv_ref[jnp.minimum(i + 1, n - 1)] > 0)

    def gather_row(ids_ref, b, r):
        j = ids_ref[0, 0, r]
        tok = jnp.where(j >= n_tok, j - n_tok, j)
        return pltpu.make_async_copy(h_ref.at[pl.ds(tok, 1)], xbuf.at[b, pl.ds(r, 1)], gsem.at[b])

    def scatter_row(ids_ref, b, r):
        return pltpu.make_async_copy(ybuf.at[b, pl.ds(r, 1)], out_ref.at[pl.ds(ids_ref[0, 0, r], 1)], ssem.at[b])

    def scatter_loop(ids_ref, b, nv):
        def issue(r, c):
            @pl.when(r < nv)
            def _():
                scatter_row(ids_ref, b, r).start()
            return c
        lax.fori_loop(0, tm, issue, 0)

    def wait_scatter(k):
        nv = tv_ref[k]
        b = k % 2

        @pl.when(nv == tm)
        def _():
            pltpu.make_async_copy(ybuf.at[b], out_ref.at[pl.ds(0, tm)], ssem.at[b]).wait()

        @pl.when(jnp.logical_and(nv > 0, nv < tm))
        def _():
            def drain(r, c):
                @pl.when(r < nv)
                def _():
                    pltpu.make_async_copy(ybuf.at[b, pl.ds(r, 1)], out_ref.at[pl.ds(0, 1)], ssem.at[b]).wait()
                return c
            lax.fori_loop(0, tm, drain, 0, unroll=8)

    @pl.when(jnp.logical_and(i == 0, nvalid > 0))
    def _():
        def issue(r, c):
            gather_row(idx_ref, 0, r).start()
            return c
        lax.fori_loop(0, tm, issue, 0)

    @pl.when(i >= 2)
    def _():
        wait_scatter(i - 2)

    @pl.when(nvalid > 0)
    def _():
        pltpu.make_async_copy(h_ref.at[pl.ds(0, tm)], xbuf.at[cur], gsem.at[cur]).wait()
        x = _unpack_bf16_pairs(xbuf[cur]).astype(BF16)

        def between(c):
            groups = n_chunks - 1
            if c >= groups:
                return
            for r in range(c * tm // groups, (c + 1) * tm // groups):
                @pl.when(next_used)
                def _():
                    gather_row(idxn_ref, 1 - cur, r).start()

                @pl.when(r < nprev)
                def _():
                    scatter_row(idxp_ref, 1 - cur, r).start(priority=1)

        ybuf[cur] = _pack_bf16_pairs(_swiglu_chunks(x, wg_ref, wu_ref, wd_ref, n_chunks, between))

    @pl.when(jnp.logical_and(nvalid == 0, nprev > 0))
    def _():
        scatter_loop(idxp_ref, 1 - cur, nprev)

    @pl.when(i == n - 1)
    def _():
        @pl.when(nvalid > 0)
        def _():
            scatter_loop(idx_ref, cur, nvalid)

        @pl.when(i >= 1)
        def _():
            wait_scatter(i - 1)
        wait_scatter(i)


def _moe_experts(tile_expert, tile_valid, ids, h2p, wg, wu, wd):
    n_tiles, _, tm = ids.shape
    n_tok = h2p.shape[0]
    once = lambda shape: pl.BlockSpec(shape, lambda i, te, tv: (te[i], 0, 0), pipeline_mode=pl.Buffered(1))
    grid_spec = pltpu.PrefetchScalarGridSpec(
        num_scalar_prefetch=2,
        grid=(n_tiles,),
        in_specs=[pl.BlockSpec((1, 1, tm), lambda i, te, tv: (i, 0, 0), memory_space=pltpu.SMEM),
                  pl.BlockSpec((1, 1, tm), lambda i, te, tv: (jnp.minimum(i + 1, n_tiles - 1), 0, 0),
                               memory_space=pltpu.SMEM),
                  pl.BlockSpec((1, 1, tm), lambda i, te, tv: (jnp.maximum(i - 1, 0), 0, 0), memory_space=pltpu.SMEM),
                  pl.BlockSpec(memory_space=pl.ANY),
                  once((1, D, D_FF)), once((1, D, D_FF)), once((1, D_FF, D))],
        out_specs=pl.BlockSpec(memory_space=pl.ANY),
        scratch_shapes=[pltpu.VMEM((2, tm, D // 2), U32), pltpu.VMEM((2, tm, D // 2), U32),
                        pltpu.SemaphoreType.DMA((2,)), pltpu.SemaphoreType.DMA((2,))],
    )
    return pl.pallas_call(
        functools.partial(_moe_expert_kernel, n_tok=n_tok, n_chunks=N_DMA_GROUPS),
        grid_spec=grid_spec,
        out_shape=jax.ShapeDtypeStruct((TOP_K * n_tok, D // 2), U32),
        compiler_params=_cp(("arbitrary",)),
        name="moe_experts",
    )(tile_expert, tile_valid, ids, ids, ids, h2p, wg, wu, wd)


def _moe_combine_kernel(y0_ref, y1_ref, rwt_ref, x_ref, g2_ref, n3_ref, o_ref):
    w = rwt_ref[...]
    f = _unpack_bf16_pairs(y0_ref[...]) * w[:, 0:1] + _unpack_bf16_pairs(y1_ref[...]) * w[:, 1:2]
    o_ref[...] = x_ref[...] + g2_ref[0] * _rms(f, n3_ref[...])


def _moe_combine(ys, rwt, x1, g2, n3, row_of, seq):
    n = x1.shape[0]
    ts = min(TM_FFN, seq)
    nper = seq // ts
    nsteps = n // ts
    rowspec = pl.BlockSpec((1, 1, D), lambda i: (row_of(i // nper), 0, 0))
    return pl.pallas_call(
        _moe_combine_kernel,
        grid=(nsteps,),
        in_specs=[pl.BlockSpec((ts, D // 2), lambda i: (i, 0)), pl.BlockSpec((ts, D // 2), lambda i: (i + nsteps, 0)),
                  pl.BlockSpec((ts, LANE), lambda i: (i, 0)), pl.BlockSpec((ts, D), lambda i: (i, 0)),
                  rowspec, _full((1, D))],
        out_specs=pl.BlockSpec((ts, D), lambda i: (i, 0)),
        out_shape=jax.ShapeDtypeStruct((n, D), F32),
        compiler_params=_cp(("arbitrary",)),
        name="moe_combine",
    )(ys, ys, rwt, x1, g2, n3)


def _route(ridx, n, tm):
    n_asg = TOP_K * n
    e_flat = jnp.concatenate([ridx[:, 0], ridx[:, 1]])
    counts = jnp.sum((e_flat[:, None] == jnp.arange(N_EXP, dtype=I32)[None, :]).astype(I32), axis=0)
    ptiles = (counts + tm - 1) // tm
    pad = ptiles * tm - counts
    fill_e = jnp.repeat(jnp.arange(N_EXP, dtype=I32), tm)
    fill_k = jnp.tile(jnp.arange(tm, dtype=I32), N_EXP)
    fill_key = jnp.where(fill_k < pad[fill_e], fill_e, N_EXP)
    keys = jnp.concatenate([e_flat, fill_key])
    n_ent = n_asg + N_EXP * tm
    shift = max(n_ent - 1, 1).bit_length()
    assert (N_EXP + 1) << shift < 2 ** 31
    order = jnp.sort((keys << shift) | jnp.arange(n_ent, dtype=I32)) & ((1 << shift) - 1)
    ids = jnp.where(order < n_asg, order, 0)
    n_tiles = n_asg // tm + N_EXP
    tile_end = jnp.cumsum(ptiles)
    t = jnp.arange(n_tiles, dtype=I32)
    te = jnp.sum((t[:, None] >= tile_end[None, :]).astype(I32), axis=1)
    used = te < N_EXP
    last_e = jnp.max(jnp.where(counts > 0, jnp.arange(N_EXP, dtype=I32), 0))
    te_c = jnp.where(used, te, last_e).astype(I32)
    start = jnp.sum(jnp.where(t[:, None] >= tile_end[None, :], ptiles[None, :], 0), axis=1)
    left = counts[jnp.minimum(te, N_EXP - 1)] - (t - start) * tm
    tv = jnp.where(used, jnp.clip(left, 0, tm), 0).astype(I32)
    return ids.reshape(n_tiles, 1, tm), te_c, tv


def _rope_tables(seq):
    rows = seq // GRID_W
    row = np.repeat(np.arange(rows, dtype=np.float64), GRID_W)
    col = np.tile(np.arange(GRID_W, dtype=np.float64), rows)
    n_freq = MLA_ROPE // 4
    inv = ROPE_BASE ** (-np.arange(n_freq, dtype=np.float64) / n_freq)
    ang = np.concatenate([row[:, None] * inv, col[:, None] * inv], axis=-1)
    cos, sin = np.cos(ang), np.sin(ang)
    cos128 = np.tile(cos, (1, 4)).astype(np.float32)
    sin128 = np.tile(sin, (1, 4)).astype(np.float32)
    sin_signed = np.tile(np.concatenate([-sin, sin], axis=-1), (1, 2)).astype(np.float32)
    return jnp.asarray(cos128), jnp.asarray(sin128), jnp.asarray(sin_signed)


def _dft_tables(seq):
    k = np.arange(seq, dtype=np.int64)
    ang = 2.0 * np.pi * ((k[:, None] * k[None, :]) % seq).astype(np.float64) / seq
    ct = (np.cos(ang) / np.sqrt(seq)).astype(np.float32)
    st = (np.sin(ang) / np.sqrt(seq)).astype(np.float32)
    c = np.arange(FNET_GD, dtype=np.int64)
    angc = 2.0 * np.pi * ((c[:, None] * c[None, :]) % FNET_GD).astype(np.float64) / FNET_GD
    eye = np.eye(FNET_G)
    cc = np.kron(eye, np.cos(angc) / np.sqrt(FNET_GD)).astype(np.float32)
    sc = np.kron(eye, np.sin(angc) / np.sqrt(FNET_GD)).astype(np.float32)
    return (jnp.asarray(ct, dtype=BF16), jnp.asarray(st, dtype=BF16), jnp.asarray(cc, dtype=BF16),
            jnp.asarray(sc, dtype=BF16))


def _rot_half_cols(w):
    return jnp.concatenate([-w[..., MLA_ROPE // 2:], w[..., :MLA_ROPE // 2]], axis=-1)


def _even_weights(w_in, w_uq, w_ukv):
    o = 3 * SC_W + MLA_QR + MLA_KVR
    kpe = w_in[:, o:o + MLA_ROPE]
    z64 = jnp.zeros((D, LANE - MLA_ROPE), w_in.dtype)
    w_in_ext = jnp.concatenate([w_in[:, :o], kpe, z64, _rot_half_cols(kpe), z64], axis=1).astype(BF16)
    wq = w_uq.reshape(MLA_QR, MLA_H, MLA_NOPE + MLA_ROPE)
    zq = jnp.zeros((MLA_QR, MLA_H, LANE - MLA_ROPE), w_uq.dtype)
    wq_main = jnp.concatenate([wq, zq], axis=-1).reshape(MLA_QR, MLA_H * 256).astype(BF16)
    wq_swap = jnp.concatenate([_rot_half_cols(wq[..., MLA_NOPE:]), zq], axis=-1).reshape(MLA_QR, MLA_H * LANE)
    return w_in_ext, wq_main, wq_swap.astype(BF16), w_ukv.astype(BF16)


def kernel(x, c, ctx, c_ctx, ev_mod_w, ev_mod_b, ev_norm_g, ev_w_in, ev_conv_w, ev_q_norm_g, ev_w_uq, ev_kv_norm_g,
           ev_w_ukv, ev_w_out, ev_ffn_gate, ev_ffn_up, ev_ffn_down, od_mod_w, od_mod_b, od_norm_g, od_w_in,
           od_lambda, od_subln_g, od_w_out, od_router, od_exp_gate, od_exp_up, od_exp_down):
    nb, seq, _ = x.shape
    ctx_len = ctx.shape[1]
    n = nb * seq
    nc = nb * ctx_len
    assert seq % GRID_W == 0 and seq % 128 == 0 and ctx_len % 128 == 0
    mod_rows = ((nb + 1 + 7) // 8) * 8
    cond = jnp.zeros((mod_rows, D), F32).at[:nb].set(c).at[nb].set(c_ctx)
    lat_row = lambda b: b
    ctx_row = lambda b: nb

    cos128, sin128, sin_signed = _rope_tables(seq)
    ones_c = jnp.ones((ctx_len, LANE), F32)
    zeros_c = jnp.zeros((ctx_len, LANE), F32)
    x2d = x.reshape(n, D)
    c2d = ctx.reshape(nc, D)

    mods = _modulation(cond, ev_mod_w[0].astype(BF16), ev_mod_b[0])
    sh1, sc1, g1, sh2, sc2, g2 = [m.reshape(mod_rows, 1, D) for m in jnp.split(mods, N_MOD, axis=-1)]
    ng = ev_norm_g[0].reshape(4, 1, D)
    w_in_e, wq_main, wq_swap, wkv = _even_weights(ev_w_in[0], ev_w_uq[0], ev_w_ukv[0])
    qg = ev_q_norm_g[0].reshape(1, MLA_QR)
    kvg = ev_kv_norm_g[0].reshape(1, MLA_KVR)
    w_out_e = ev_w_out[0].astype(BF16)
    wg_e, wu_e, wd_e = ev_ffn_gate[0].astype(BF16), ev_ffn_up[0].astype(BF16), ev_ffn_down[0].astype(BF16)

    bg_l, uc_l, q_l, k_l, v_l = _front_even(x2d, sh1, sc1, lat_row, ng[0], w_in_e, qg, wq_main, wq_swap, kvg, wkv,
                                            cos128, sin128, seq)
    bg_c, uc_c, q_c, k_c, v_c = _front_even(c2d, sh1, sc1, ctx_row, ng[0], w_in_e, qg, wq_main, wq_swap, kvg, wkv,
                                            ones_c, zeros_c, ctx_len)
    at_l = _mla_attn(q_l, k_c, v_c, k_l, v_l, nb, seq, ctx_len)
    at_c = _mla_attn(q_c, k_c, v_c, None, None, nb, ctx_len, ctx_len)
    x1_l, h2_l = _mix_even(bg_l, uc_l, ev_conv_w[0], at_l, w_out_e, x2d, g1, ng[1], ng[2], sh2, sc2, lat_row, seq)
    x1_c, h2_c = _mix_even(bg_c, uc_c, ev_conv_w[0], at_c, w_out_e, c2d, g1, ng[1], ng[2], sh2, sc2, ctx_row,
                           ctx_len)
    x2d = _ffn_dense(h2_l, wg_e, wu_e, wd_e, x1_l, g2, ng[3], lat_row, seq)
    c2d = _ffn_dense(h2_c, wg_e, wu_e, wd_e, x1_c, g2, ng[3], ctx_row, ctx_len)

    lam_init = 0.8 - 0.6 * math.exp(-0.3 * 1)
    mods = _modulation(cond, od_mod_w[0].astype(BF16), od_mod_b[0])
    sh1, sc1, g1, sh2, sc2, g2 = [m.reshape(mod_rows, 1, D) for m in jnp.split(mods, N_MOD, axis=-1)]
    ng = od_norm_g[0].reshape(4, 1, D)
    w_in_o = od_w_in[0].astype(BF16)
    q_o, k_o, v_o, f_o = _front_odd(x2d, sh1, sc1, lat_row, ng[0], w_in_o, cos128, sin_signed, seq, True)
    kc_o, vc_o = _front_odd(c2d, sh1, sc1, ctx_row, ng[0], w_in_o[:, DIFF_W:3 * DIFF_W], ones_c, zeros_c, ctx_len,
                            False)
    ca = _diff_attn(q_o, kc_o, vc_o, k_o, v_o, od_lambda[0], od_subln_g[0].reshape(1, 2 * DIFF_HD), nb, seq,
                    ctx_len, lam_init)
    ct, st, cc, sc = _dft_tables(seq)
    fd = _fourier(f_o, ct, st, cc, sc, nb, seq)
    router_f = jnp.zeros((D, LANE), F32).at[:, :N_EXP].set(od_router[0])
    router_hi = router_f.astype(BF16)
    router_pad = jnp.concatenate([router_hi, (router_f - router_hi.astype(F32)).astype(BF16)], axis=1)
    x1, h2p, ridx, rwt = _mix_odd(ca, fd, od_w_out[0].astype(BF16), x2d, g1, ng[1], ng[2], sh2, sc2, router_pad,
                                  lat_row, seq)
    ids, tile_expert, tile_valid = _route(ridx, n, TM_EXP)
    ys = _moe_experts(tile_expert, tile_valid, ids, h2p, od_exp_gate[0].astype(BF16), od_exp_up[0].astype(BF16),
                      od_exp_down[0].astype(BF16))
    out = _moe_combine(ys, rwt, x1, g2, ng[3], lat_row, seq)
    return out.reshape(nb, seq, D)
```

```python
import functools
import math

import numpy as np
import jax
import jax.numpy as jnp
from jax import lax
from jax.experimental import pallas as pl
from jax.experimental.pallas import tpu as pltpu

F32 = jnp.float32
BF16 = jnp.bfloat16
I32 = jnp.int32

D = 1024
GRID_W = 64
EPS = 1e-6
ROPE_BASE = 10000.0
N_MOD = 6
SC_W = D // 2
MLA_V = 128
MLA_NOPE = 128
MLA_ROPE = 64
MLA_H = (D - SC_W) // MLA_V
MLA_QR = 3 * D // 8
MLA_KVR = D // 4
MLA_SCALE = (MLA_NOPE + MLA_ROPE) ** -0.5
DIFF_W = 3 * D // 4
DIFF_HD = 64
DIFF_H = DIFF_W // (2 * DIFF_HD)
DIFF_SCALE = DIFF_HD ** -0.5
FNET_W = D - DIFF_W
FNET_G = 4
FNET_GD = FNET_W // FNET_G
D_FF = ((8 * D // 3 + 127) // 128) * 128
N_EXP = 8
TOP_K = 2
LOG2E = math.log2(math.e)

LANE = 128
VMEM_LIMIT = 56 * 1024 * 1024

TM = 1024
TM_FFN = 512
N_ROW_SPLIT = 4
TQ_MLA = 1024
TQ_DIFF = 256
BK_ATTN = 256
HP_DIFF = 3
TK_FFT = 512
TM_EXP = 512
N_DMA_GROUPS = 2


def _cp(sem, vmem=VMEM_LIMIT):
    return pltpu.CompilerParams(dimension_semantics=sem, vmem_limit_bytes=vmem)


def _rms(x, g):
    return x * lax.rsqrt(jnp.mean(x * x, axis=-1, keepdims=True) + EPS) * g


def _dot(a, b):
    return jnp.dot(a, b, preferred_element_type=F32)


def _dot_nt(a, b):
    return lax.dot_general(a, b, (((1,), (1,)), ((), ())), preferred_element_type=F32)


def _full(shape):
    nd = len(shape)
    return pl.BlockSpec(shape, lambda *_: (0,) * nd)


def _mod_kernel(c_ref, w_ref, b_ref, o_ref):
    c = c_ref[...]
    s = c / (1.0 + jnp.exp(-c))
    o_ref[...] = _dot(s.astype(BF16), w_ref[...]) + b_ref[...]


def _modulation(cond, w_bf, b):
    rows = cond.shape[0]
    n = w_bf.shape[1]
    tn = 1536
    return pl.pallas_call(
        _mod_kernel,
        grid=(n // tn,),
        in_specs=[_full((rows, D)), pl.BlockSpec((D, tn), lambda j: (0, j)), pl.BlockSpec((1, tn), lambda j: (0, j))],
        out_specs=pl.BlockSpec((rows, tn), lambda j: (0, j)),
        out_shape=jax.ShapeDtypeStruct((rows, n), F32),
        compiler_params=_cp(("arbitrary",)),
        name="modulation",
    )(cond, w_bf, b.reshape(1, n))


def _front_even_kernel(x_ref, sh_ref, sc_ref, g0_ref, win_ref, qg_ref, wq_ref, wqs_ref, kvg_ref, wkv_ref,
                       cos_ref, sin_ref, bg_ref, uc_ref, q_ref, k_ref, v_ref):
    x = x_ref[...]
    h = _rms(x, g0_ref[...] * (1.0 + sc_ref[0])) + sh_ref[0]
    z = _dot(h.astype(BF16), win_ref[...])
    bg_ref[...] = z[:, 0:SC_W].astype(BF16)
    uc_ref[...] = (z[:, SC_W:2 * SC_W] * z[:, 2 * SC_W:3 * SC_W]).astype(BF16)
    o = 3 * SC_W
    zq = z[:, o:o + MLA_QR]
    zkv = z[:, o + MLA_QR:o + MLA_QR + MLA_KVR]
    o2 = o + MLA_QR + MLA_KVR
    kpe = z[:, o2:o2 + LANE]
    kpes = z[:, o2 + LANE:o2 + 2 * LANE]
    cos = cos_ref[...]
    sin = sin_ref[...]
    zqn = _rms(zq, qg_ref[...]).astype(BF16)
    qm = _dot(zqn, wq_ref[...])
    qs = _dot(zqn, wqs_ref[...])
    qscale = MLA_SCALE * LOG2E
    for hd in range(MLA_H):
        lo = qm[:, 256 * hd:256 * hd + LANE]
        hi = qm[:, 256 * hd + LANE:256 * hd + 2 * LANE] * cos + qs[:, LANE * hd:LANE * hd + LANE] * sin
        q_ref[hd, :, 0:LANE] = (lo * qscale).astype(BF16)
        q_ref[hd, :, LANE:2 * LANE] = (hi * qscale).astype(BF16)
    zkvn = _rms(zkv, kvg_ref[...]).astype(BF16)
    kv = _dot(zkvn, wkv_ref[...])
    kpr = (kpe * cos + kpes * sin).astype(BF16)
    for hd in range(MLA_H):
        k_ref[hd, :, 0:LANE] = kv[:, 256 * hd:256 * hd + LANE].astype(BF16)
        k_ref[hd, :, LANE:2 * LANE] = kpr
        v_ref[hd] = kv[:, 256 * hd + LANE:256 * hd + 2 * LANE].astype(BF16)


def _front_even(x2d, sh, sc, row_of, g0, w_in, qg, wq, wqs, kvg, wkv, cos, sin, seq):
    n = x2d.shape[0]
    tm = min(TM, seq)
    nper = seq // tm
    rowspec = pl.BlockSpec((1, 1, D), lambda i: (row_of(i // nper), 0, 0))
    tabspec = pl.BlockSpec((tm, LANE), lambda i: (i % nper, 0))
    win_n = w_in.shape[1]
    return pl.pallas_call(
        _front_even_kernel,
        grid=(n // tm,),
        in_specs=[pl.BlockSpec((tm, D), lambda i: (i, 0)), rowspec, rowspec, _full((1, D)), _full((D, win_n)),
                  _full((1, MLA_QR)), _full((MLA_QR, 4 * 256)), _full((MLA_QR, 4 * LANE)),
                  _full((1, MLA_KVR)), _full((MLA_KVR, 4 * 256)), tabspec, tabspec],
        out_specs=[pl.BlockSpec((tm, SC_W), lambda i: (i, 0)), pl.BlockSpec((tm, SC_W), lambda i: (i, 0)),
                   pl.BlockSpec((MLA_H, tm, 256), lambda i: (0, i, 0)),
                   pl.BlockSpec((MLA_H, tm, 256), lambda i: (0, i, 0)),
                   pl.BlockSpec((MLA_H, tm, LANE), lambda i: (0, i, 0))],
        out_shape=[jax.ShapeDtypeStruct((n, SC_W), BF16), jax.ShapeDtypeStruct((n, SC_W), BF16),
                   jax.ShapeDtypeStruct((MLA_H, n, 256), BF16), jax.ShapeDtypeStruct((MLA_H, n, 256), BF16),
                   jax.ShapeDtypeStruct((MLA_H, n, LANE), BF16)],
        compiler_params=_cp(("parallel",)),
        name="front_even",
    )(x2d, sh, sc, g0, w_in, qg, wq, wqs, kvg, wkv, cos, sin)


def _key_blocks(k_ref, v_ref, bk):
    n = k_ref.shape[0]
    return [(k_ref.at[pl.ds(j, min(bk, n - j))], v_ref.at[pl.ds(j, min(bk, n - j))]) for j in range(0, n, bk)]


def _online_softmax_pv(q, blocks):
    m = acc = None
    dv = blocks[0][1].shape[1]
    for k_blk, v_blk in blocks:
        bk = k_blk.shape[0]
        ones_col = (lax.broadcasted_iota(I32, (bk, LANE), 1) == 0).astype(BF16)
        v_ext = jnp.concatenate([v_blk[...], ones_col], axis=1)
        s = _dot_nt(q, k_blk[...])
        mb = jnp.max(s, axis=-1, keepdims=True)
        if m is None:
            m = mb
            acc = _dot(jnp.exp2(s - m).astype(BF16), v_ext)
        else:
            m_new = jnp.maximum(m, mb)
            acc = jnp.exp2(m - m_new) * acc + _dot(jnp.exp2(s - m_new).astype(BF16), v_ext)
            m = m_new
    return acc[:, :dv], acc[:, dv:dv + 1]


def _mla_attn_kernel(*refs, with_lat, bk):
    if with_lat:
        q_ref, kc_ref, vc_ref, kl_ref, vl_ref, o_ref = refs
    else:
        q_ref, kc_ref, vc_ref, o_ref = refs
    blocks = _key_blocks(kc_ref.at[0], vc_ref.at[0], bk)
    if with_lat:
        blocks += _key_blocks(kl_ref.at[0], vl_ref.at[0], bk)
    acc, l = _online_softmax_pv(q_ref[0], blocks)
    o_ref[...] = (acc * (1.0 / l)).astype(BF16)


def _mla_attn(q, k_ctx, v_ctx, k_lat, v_lat, nb, seq_q, ctx_len):
    with_lat = k_lat is not None
    tq = min(TQ_MLA, seq_q)
    nq = seq_q // tq
    in_specs = [pl.BlockSpec((1, tq, 256), lambda b, h, i: (h, b * nq + i, 0)),
                pl.BlockSpec((1, ctx_len, 256), lambda b, h, i: (h, b, 0)),
                pl.BlockSpec((1, ctx_len, LANE), lambda b, h, i: (h, b, 0))]
    args = [q, k_ctx, v_ctx]
    if with_lat:
        in_specs += [pl.BlockSpec((1, seq_q, 256), lambda b, h, i: (h, b, 0)),
                     pl.BlockSpec((1, seq_q, LANE), lambda b, h, i: (h, b, 0))]
        args += [k_lat, v_lat]
    return pl.pallas_call(
        functools.partial(_mla_attn_kernel, with_lat=with_lat, bk=BK_ATTN),
        grid=(nb, MLA_H, nq),
        in_specs=in_specs,
        out_specs=pl.BlockSpec((tq, LANE), lambda b, h, i: (b * nq + i, h)),
        out_shape=jax.ShapeDtypeStruct((nb * seq_q, MLA_H * MLA_V), BF16),
        compiler_params=_cp(("parallel", "parallel", "arbitrary")),
        name="mla_attn_lat" if with_lat else "mla_attn_ctx",
    )(*args)


def _residual_and_h2(y, x, g1_ref, n1_ref, n2_ref, sh2_ref, sc2_ref):
    x1 = x + _rms(y, g1_ref[0] * n1_ref[...])
    h2 = _rms(x1, n2_ref[...] * (1.0 + sc2_ref[0])) + sh2_ref[0]
    return x1, h2


def _mix_even_kernel(bg_ref, uc_ref, ucp_ref, ucn_ref, cw_ref, at_ref, wo_ref, x_ref, g1_ref, n1_ref, n2_ref,
                     sh2_ref, sc2_ref, x1_ref, h2_ref, scr, *, nper):
    tm = uc_ref.shape[0]
    i = pl.program_id(0)
    ucf = uc_ref[...].astype(F32)
    first = (i % nper) == 0
    last = (i % nper) == nper - 1
    prev_row = jnp.where(first, 0.0, ucp_ref[7:8, :].astype(F32))
    next_row = jnp.where(last, 0.0, ucn_ref[0:1, :].astype(F32))
    scr[8:8 + tm, :] = ucf
    scr[7:8, :] = prev_row
    scr[8 + tm:9 + tm, :] = next_row
    up = scr[7:7 + tm, :]
    dn = scr[9:9 + tm, :]
    conv = cw_ref[0:1, :] * up + cw_ref[1:2, :] * ucf + cw_ref[2:3, :] * dn
    a = (bg_ref[...].astype(F32) * conv).astype(BF16)
    rows = tm // N_ROW_SPLIT
    for p in range(N_ROW_SPLIT):
        rs = slice(p * rows, (p + 1) * rows)
        y = _dot(a[rs], wo_ref[0:SC_W, :]) + _dot(at_ref[rs, :], wo_ref[SC_W:D, :])
        x1, h2 = _residual_and_h2(y, x_ref[rs, :], g1_ref, n1_ref, n2_ref, sh2_ref, sc2_ref)
        x1_ref[rs, :] = x1
        h2_ref[rs, :] = h2.astype(BF16)


def _mix_even(bg, uc, conv_w, attn, w_out, x2d, g1, n1, n2, sh2, sc2, row_of, seq):
    n = x2d.shape[0]
    tm = min(TM, seq)
    nper = seq // tm
    nb8 = n // 8
    rowspec = pl.BlockSpec((1, 1, D), lambda i: (row_of(i // nper), 0, 0))
    tile = lambda w: pl.BlockSpec((tm, w), lambda i: (i, 0))
    return pl.pallas_call(
        functools.partial(_mix_even_kernel, nper=nper),
        grid=(n // tm,),
        in_specs=[tile(SC_W), tile(SC_W),
                  pl.BlockSpec((8, SC_W), lambda i: (jnp.maximum(i * (tm // 8) - 1, 0), 0)),
                  pl.BlockSpec((8, SC_W), lambda i: (jnp.minimum((i + 1) * (tm // 8), nb8 - 1), 0)),
                  _full((3, SC_W)), tile(MLA_H * MLA_V), _full((D, D)), tile(D), rowspec, _full((1, D)),
                  _full((1, D)), rowspec, rowspec],
        out_specs=[tile(D), tile(D)],
        out_shape=[jax.ShapeDtypeStruct((n, D), F32), jax.ShapeDtypeStruct((n, D), BF16)],
        scratch_shapes=[pltpu.VMEM((tm + 16, SC_W), F32)],
        compiler_params=_cp(("parallel",)),
        name="mix_even",
    )(bg, uc, uc, uc, conv_w, attn, w_out, x2d, g1, n1, n2, sh2, sc2)


def _mix_odd_kernel(ca_ref, fd_ref, wo_ref, x_ref, g1_ref, n1_ref, n2_ref, sh2_ref, sc2_ref, rw_ref,
                    x1_ref, h2p_ref, ridx_ref, rwt_ref):
    rows = x_ref.shape[0] // N_ROW_SPLIT
    h2s = []
    for p in range(N_ROW_SPLIT):
        rs = slice(p * rows, (p + 1) * rows)
        y = _dot(ca_ref[rs, :], wo_ref[0:DIFF_W, :]) + _dot(fd_ref[rs, :], wo_ref[DIFF_W:D, :])
        x1, h2g = _residual_and_h2(y, x_ref[rs, :], g1_ref, n1_ref, n2_ref, sh2_ref, sc2_ref)
        x1_ref[rs, :] = x1
        h2p_ref[rs, :] = h2g
        h2s.append(h2g)
    h2 = jnp.concatenate(h2s, axis=0)
    tm = h2.shape[0]
    hi = h2.astype(BF16)
    lo = (h2 - hi.astype(F32)).astype(BF16)
    r = _dot(jnp.concatenate([hi, lo], axis=0), rw_ref[...])
    logits = (r[:tm, :LANE] + r[:tm, LANE:]) + (r[tm:, :LANE] + r[tm:, LANE:])
    lane = lax.broadcasted_iota(I32, logits.shape, 1).astype(F32)
    neg = jnp.float32(-jnp.inf)
    s0 = jnp.where(lane < N_EXP, logits, neg)
    m1 = jnp.max(s0, axis=-1, keepdims=True)
    i1 = jnp.min(jnp.where(s0 == m1, lane, float(LANE)), axis=-1, keepdims=True)
    s1 = jnp.where(lane == i1, neg, s0)
    m2 = jnp.max(s1, axis=-1, keepdims=True)
    i2 = jnp.min(jnp.where(s1 == m2, lane, float(LANE)), axis=-1, keepdims=True)
    e = jnp.exp(m2 - m1)
    w1 = 1.0 / (1.0 + e)
    w2 = e * w1
    ridx_ref[...] = jnp.where(lane == 0.0, i1, jnp.where(lane == 1.0, i2, 0.0)).astype(I32)
    rwt_ref[...] = jnp.where(lane == 0.0, w1, jnp.where(lane == 1.0, w2, 0.0))


def _mix_odd(cattn, fd, w_out, x2d, g1, n1, n2, sh2, sc2, router_pad, row_of, seq):
    n = x2d.shape[0]
    tm = min(TM, seq)
    nper = seq // tm
    rowspec = pl.BlockSpec((1, 1, D), lambda i: (row_of(i // nper), 0, 0))
    tile = lambda w: pl.BlockSpec((tm, w), lambda i: (i, 0))
    return pl.pallas_call(
        _mix_odd_kernel,
        grid=(n // tm,),
        in_specs=[tile(DIFF_W), tile(FNET_W), _full((D, D)), tile(D), rowspec, _full((1, D)), _full((1, D)),
                  rowspec, rowspec, _full((D, 2 * LANE))],
        out_specs=[tile(D), tile(D), tile(LANE), tile(LANE)],
        out_shape=[jax.ShapeDtypeStruct((n, D), F32), jax.ShapeDtypeStruct((n, D), F32),
                   jax.ShapeDtypeStruct((n, LANE), I32), jax.ShapeDtypeStruct((n, LANE), F32)],
        compiler_params=_cp(("parallel",)),
        name="mix_odd",
    )(cattn, fd, w_out, x2d, g1, n1, n2, sh2, sc2, router_pad)


def _swiglu(h, wg, wu, wd):
    g = _dot(h, wg)
    u = _dot(h, wu)
    a = (g / (1.0 + jnp.exp(-g)) * u).astype(BF16)
    return _dot(a, wd)


def _ffn_kernel(h_ref, wg_ref, wu_ref, wd_ref, x_ref, g2_ref, n3_ref, o_ref):
    f = _swiglu(h_ref[...], wg_ref[...], wu_ref[...], wd_ref[...])
    o_ref[...] = x_ref[...] + g2_ref[0] * _rms(f, n3_ref[...])


def _ffn_dense(h2, wg, wu, wd, x1, g2, n3, row_of, seq):
    n = x1.shape[0]
    tm = min(TM_FFN, seq)
    nper = seq // tm
    rowspec = pl.BlockSpec((1, 1, D), lambda i: (row_of(i // nper), 0, 0))
    tile = lambda w: pl.BlockSpec((tm, w), lambda i: (i, 0))
    once = lambda shape: pl.BlockSpec(shape, lambda i: (0, 0), pipeline_mode=pl.Buffered(1))
    return pl.pallas_call(
        _ffn_kernel,
        grid=(n // tm,),
        in_specs=[tile(D), once((D, D_FF)), once((D, D_FF)), once((D_FF, D)), tile(D), rowspec, _full((1, D))],
        out_specs=tile(D),
        out_shape=jax.ShapeDtypeStruct((n, D), F32),
        compiler_params=_cp(("parallel",)),
        name="ffn_dense",
    )(h2, wg, wu, wd, x1, g2, n3)


def _rope_slab(x, cos, sin_signed, lane):
    swap = jnp.where((lane & 63) < 32, pltpu.roll(x, 96, 1), pltpu.roll(x, 32, 1))
    return x * cos + swap * sin_signed


def _front_odd_kernel(x_ref, sh_ref, sc_ref, g0_ref, win_ref, cos_ref, sin_ref, q_ref, k_ref, v_ref, f_ref,
                      *, with_q):
    x = x_ref[...]
    h = _rms(x, g0_ref[...] * (1.0 + sc_ref[0])) + sh_ref[0]
    z = _dot(h.astype(BF16), win_ref[...])
    cos = cos_ref[...]
    sin = sin_ref[...]
    lane = lax.broadcasted_iota(I32, cos.shape, 1)
    off = DIFF_W if with_q else 0
    qscale = DIFF_SCALE * LOG2E
    for g in range(DIFF_W // LANE):
        sl = slice(LANE * g, LANE * g + LANE)
        if with_q:
            q_ref[:, sl] = (_rope_slab(z[:, sl], cos, sin, lane) * qscale).astype(BF16)
        ksl = slice(off + LANE * g, off + LANE * g + LANE)
        k_ref[:, sl] = _rope_slab(z[:, ksl], cos, sin, lane).astype(BF16)
    v_ref[...] = z[:, off + DIFF_W:off + 2 * DIFF_W].astype(BF16)
    if with_q:
        f_ref[...] = z[:, 3 * DIFF_W:3 * DIFF_W + FNET_W].astype(BF16)


def _front_odd(x2d, sh, sc, row_of, g0, w_in, cos, sin, seq, with_q):
    n = x2d.shape[0]
    tm = min(TM, seq)
    nper = seq // tm
    rowspec = pl.BlockSpec((1, 1, D), lambda i: (row_of(i // nper), 0, 0))
    tabspec = pl.BlockSpec((tm, LANE), lambda i: (i % nper, 0))
    tile = lambda w: pl.BlockSpec((tm, w), lambda i: (i, 0))
    if with_q:
        kern = functools.partial(_front_odd_kernel, with_q=True)
        out_specs = [tile(DIFF_W), tile(DIFF_W), tile(DIFF_W), tile(FNET_W)]
        out_shape = [jax.ShapeDtypeStruct((n, DIFF_W), BF16)] * 3 + [jax.ShapeDtypeStruct((n, FNET_W), BF16)]
    else:
        def kern(x_ref, sh_ref, sc_ref, g0_ref, win_ref, cos_ref, sin_ref, k_ref, v_ref):
            _front_odd_kernel(x_ref, sh_ref, sc_ref, g0_ref, win_ref, cos_ref, sin_ref, None, k_ref, v_ref, None,
                              with_q=False)
        out_specs = [tile(DIFF_W), tile(DIFF_W)]
        out_shape = [jax.ShapeDtypeStruct((n, DIFF_W), BF16)] * 2
    return pl.pallas_call(
        kern,
        grid=(n // tm,),
        in_specs=[tile(D), rowspec, rowspec, _full((1, D)), _full((D, w_in.shape[1])), tabspec, tabspec],
        out_specs=out_specs,
        out_shape=out_shape,
        compiler_params=_cp(("parallel",)),
        name="front_odd" if with_q else "front_odd_ctx",
    )(x2d, sh, sc, g0, w_in, cos, sin)


def _diff_attn_kernel(q_ref, kc_ref, vc_ref, kl_ref, vl_ref, lam_ref, sg_ref, o_ref, *, lam_init, bk):
    tq = q_ref.shape[0]
    lp = lam_ref[...]
    lam = (jnp.exp(jnp.sum(lp[0:1, :] * lp[1:2, :], axis=-1, keepdims=True))
           - jnp.exp(jnp.sum(lp[2:3, :] * lp[3:4, :], axis=-1, keepdims=True)) + lam_init)
    lane = lax.broadcasted_iota(I32, (tq, LANE), 1)
    for hd in range(q_ref.shape[1] // LANE):
        sl = pl.ds(hd * LANE, LANE)
        q = q_ref[:, sl]
        zero = jnp.zeros_like(q)
        qq = jnp.concatenate([jnp.where(lane < DIFF_HD, q, zero), jnp.where(lane >= DIFF_HD, q, zero)], axis=0)
        blocks = (_key_blocks(kc_ref.at[:, sl], vc_ref.at[:, sl], bk)
                  + _key_blocks(kl_ref.at[:, sl], vl_ref.at[:, sl], bk))
        acc, l = _online_softmax_pv(qq, blocks)
        r = 1.0 / l
        o = acc[:tq] * r[:tq] - acc[tq:] * (r[tq:] * lam)
        o_ref[:, sl] = (_rms(o, sg_ref[...]) * (1.0 - lam_init)).astype(BF16)


def _diff_attn(q, k_ctx, v_ctx, k_lat, v_lat, lam_p, subln_g, nb, seq, ctx_len, lam_init):
    tq = min(TQ_DIFF, seq)
    nq = seq // tq
    return pl.pallas_call(
        functools.partial(_diff_attn_kernel, lam_init=lam_init, bk=BK_ATTN),
        grid=(nb, DIFF_H // HP_DIFF, nq),
        in_specs=[pl.BlockSpec((tq, HP_DIFF * LANE), lambda b, h, i: (b * nq + i, h)),
                  pl.BlockSpec((ctx_len, HP_DIFF * LANE), lambda b, h, i: (b, h)),
                  pl.BlockSpec((ctx_len, HP_DIFF * LANE), lambda b, h, i: (b, h)),
                  pl.BlockSpec((seq, HP_DIFF * LANE), lambda b, h, i: (b, h)),
                  pl.BlockSpec((seq, HP_DIFF * LANE), lambda b, h, i: (b, h)),
                  _full((4, DIFF_HD)), _full((1, 2 * DIFF_HD))],
        out_specs=pl.BlockSpec((tq, HP_DIFF * LANE), lambda b, h, i: (b * nq + i, h)),
        out_shape=jax.ShapeDtypeStruct((nb * seq, DIFF_W), BF16),
        compiler_params=_cp(("parallel", "parallel", "arbitrary")),
        name="diff_attn",
    )(q, k_ctx, v_ctx, k_lat, v_lat, lam_p, subln_g)


def _fourier_kernel(ct_ref, st_ref, f_ref, cc_ref, sc_ref, o_ref):
    f = f_ref[...]
    p = _dot(ct_ref[...], f).astype(BF16)
    q = _dot(st_ref[...], f).astype(BF16)
    o_ref[...] = (_dot(p, cc_ref[...]) - _dot(q, sc_ref[...])).astype(BF16)


def _fourier(f2d, ct, st, cc, sc, nb, seq):
    tk = min(TK_FFT, seq)
    nk = seq // tk
    return pl.pallas_call(
        _fourier_kernel,
        grid=(nk, nb),
        in_specs=[pl.BlockSpec((tk, seq), lambda j, b: (j, 0)), pl.BlockSpec((tk, seq), lambda j, b: (j, 0)),
                  pl.BlockSpec((seq, FNET_W), lambda j, b: (b, 0)), _full((FNET_W, FNET_W)), _full((FNET_W, FNET_W))],
        out_specs=pl.BlockSpec((tk, FNET_W), lambda j, b: (b * nk + j, 0)),
        out_shape=jax.ShapeDtypeStruct((nb * seq, FNET_W), BF16),
        compiler_params=_cp(("arbitrary", "arbitrary")),
        name="fourier",
    )(ct, st, f2d, cc, sc)


def _swiglu_chunks(x, wg_ref, wu_ref, wd_ref, n_chunks, between):
    f = wg_ref.shape[2]
    step = -(-f // (n_chunks * 256)) * 256
    acc = None
    for c in range(n_chunks):
        lo, hi = c * step, min((c + 1) * step, f)
        g = _dot(x, wg_ref[0, :, lo:hi])
        u = _dot(x, wu_ref[0, :, lo:hi])
        a = (g / (1.0 + jnp.exp(-g)) * u).astype(BF16)
        part = _dot(a, wd_ref[0, lo:hi, :])
        acc = part if acc is None else acc + part
        between(c)
    return acc


def _moe_expert_kernel(te_ref, tv_ref, idx_ref, idxn_ref, idxp_ref, h_ref, wg_ref, wu_ref, wd_ref, out_ref,
                       xbuf, ybuf, gsem, ssem, *, n_tok, n_chunks):
    tm = xbuf.shape[1]
    i = pl.program_id(0)
    n = pl.num_programs(0)
    cur = i % 2
    nvalid = tv_ref[i]
    nprev = jnp.where(i >= 1, tv_ref[jnp.maximum(i - 1, 0)], 0)
    next_used = jnp.logical_and(i + 1 < n, tv_ref[jnp.minimum(i + 1, n - 1)] > 0)

    def gather_row(ids_ref, b, r):
        j = ids_ref[0, 0, r]
        tok = jnp.where(j >= n_tok, j - n_tok, j)
        return pltpu.make_async_copy(h_ref.at[pl.ds(tok, 1)], xbuf.at[b, pl.ds(r, 1)], gsem.at[b])

    def scatter_row(ids_ref, b, r):
        return pltpu.make_async_copy(ybuf.at[b, pl.ds(r, 1)], out_ref.at[pl.ds(ids_ref[0, 0, r], 1)], ssem.at[b])

    def scatter_loop(ids_ref, b, nv):
        def issue(r, c):
            @pl.when(r < nv)
            def _():
                scatter_row(ids_ref, b, r).start()
            return c
        lax.fori_loop(0, tm, issue, 0)

    def wait_scatter(k):
        nv = tv_ref[k]
        b = k % 2

        @pl.when(nv == tm)
        def _():
            pltpu.make_async_copy(ybuf.at[b], out_ref.at[pl.ds(0, tm)], ssem.at[b]).wait()

        @pl.when(jnp.logical_and(nv > 0, nv < tm))
        def _():
            def drain(r, c):
                @pl.when(r < nv)
                def _():
                    pltpu.make_async_copy(ybuf.at[b, pl.ds(r, 1)], out_ref.at[pl.ds(0, 1)], ssem.at[b]).wait()
                return c
            lax.fori_loop(0, tm, drain, 0, unroll=8)

    @pl.when(jnp.logical_and(i == 0, nvalid > 0))
    def _():
        def issue(r, c):
            gather_row(idx_ref, 0, r).start()
            return c
        lax.fori_loop(0, tm, issue, 0)

    @pl.when(i >= 2)
    def _():
        wait_scatter(i - 2)

    @pl.when(nvalid > 0)
    def _():
        pltpu.make_async_copy(h_ref.at[pl.ds(0, tm)], xbuf.at[cur], gsem.at[cur]).wait()
        x = xbuf[cur].astype(BF16)

        def between(c):
            groups = n_chunks - 1
            if c >= groups:
                return
            for r in range(c * tm // groups, (c + 1) * tm // groups):
                @pl.when(next_used)
                def _():
                    gather_row(idxn_ref, 1 - cur, r).start()

                @pl.when(r < nprev)
                def _():
                    scatter_row(idxp_ref, 1 - cur, r).start(priority=1)

        ybuf[cur] = _swiglu_chunks(x, wg_ref, wu_ref, wd_ref, n_chunks, between)

    @pl.when(jnp.logical_and(nvalid == 0, nprev > 0))
    def _():
        scatter_loop(idxp_ref, 1 - cur, nprev)

    @pl.when(i == n - 1)
    def _():
        @pl.when(nvalid > 0)
        def _():
            scatter_loop(idx_ref, cur, nvalid)

        @pl.when(i >= 1)
        def _():
            wait_scatter(i - 1)
        wait_scatter(i)


def _moe_experts(tile_expert, tile_valid, ids, h2p, wg, wu, wd):
    n_tiles, _, tm = ids.shape
    n_tok = h2p.shape[0]
    once = lambda shape: pl.BlockSpec(shape, lambda i, te, tv: (te[i], 0, 0), pipeline_mode=pl.Buffered(1))
    grid_spec = pltpu.PrefetchScalarGridSpec(
        num_scalar_prefetch=2,
        grid=(n_tiles,),
        in_specs=[pl.BlockSpec((1, 1, tm), lambda i, te, tv: (i, 0, 0), memory_space=pltpu.SMEM),
                  pl.BlockSpec((1, 1, tm), lambda i, te, tv: (jnp.minimum(i + 1, n_tiles - 1), 0, 0),
                               memory_space=pltpu.SMEM),
                  pl.BlockSpec((1, 1, tm), lambda i, te, tv: (jnp.maximum(i - 1, 0), 0, 0), memory_space=pltpu.SMEM),
                  pl.BlockSpec(memory_space=pl.ANY),
                  once((1, D, D_FF)), once((1, D, D_FF)), once((1, D_FF, D))],
        out_specs=pl.BlockSpec(memory_space=pl.ANY),
        scratch_shapes=[pltpu.VMEM((2, tm, D), F32), pltpu.VMEM((2, tm, D), F32),
                        pltpu.SemaphoreType.DMA((2,)), pltpu.SemaphoreType.DMA((2,))],
    )
    return pl.pallas_call(
        functools.partial(_moe_expert_kernel, n_tok=n_tok, n_chunks=N_DMA_GROUPS),
        grid_spec=grid_spec,
        out_shape=jax.ShapeDtypeStruct((TOP_K * n_tok, D), F32),
        compiler_params=_cp(("arbitrary",)),
        name="moe_experts",
    )(tile_expert, tile_valid, ids, ids, ids, h2p, wg, wu, wd)


def _moe_combine_kernel(y0_ref, y1_ref, rwt_ref, x_ref, g2_ref, n3_ref, o_ref):
    w = rwt_ref[...]
    f = y0_ref[...] * w[:, 0:1] + y1_ref[...] * w[:, 1:2]
    o_ref[...] = x_ref[...] + g2_ref[0] * _rms(f, n3_ref[...])


def _moe_combine(ys, rwt, x1, g2, n3, row_of, seq):
    n = x1.shape[0]
    ts = min(TM_FFN, seq)
    nper = seq // ts
    nsteps = n // ts
    rowspec = pl.BlockSpec((1, 1, D), lambda i: (row_of(i // nper), 0, 0))
    return pl.pallas_call(
        _moe_combine_kernel,
        grid=(nsteps,),
        in_specs=[pl.BlockSpec((ts, D), lambda i: (i, 0)), pl.BlockSpec((ts, D), lambda i: (i + nsteps, 0)),
                  pl.BlockSpec((ts, LANE), lambda i: (i, 0)), pl.BlockSpec((ts, D), lambda i: (i, 0)),
                  rowspec, _full((1, D))],
        out_specs=pl.BlockSpec((ts, D), lambda i: (i, 0)),
        out_shape=jax.ShapeDtypeStruct((n, D), F32),
        compiler_params=_cp(("arbitrary",)),
        name="moe_combine",
    )(ys, ys, rwt, x1, g2, n3)


def _route(ridx, n, tm):
    n_asg = TOP_K * n
    e_flat = jnp.concatenate([ridx[:, 0], ridx[:, 1]])
    counts = jnp.sum((e_flat[:, None] == jnp.arange(N_EXP, dtype=I32)[None, :]).astype(I32), axis=0)
    ptiles = (counts + tm - 1) // tm
    pad = ptiles * tm - counts
    fill_e = jnp.repeat(jnp.arange(N_EXP, dtype=I32), tm)
    fill_k = jnp.tile(jnp.arange(tm, dtype=I32), N_EXP)
    fill_key = jnp.where(fill_k < pad[fill_e], fill_e, N_EXP)
    keys = jnp.concatenate([e_flat, fill_key])
    n_ent = n_asg + N_EXP * tm
    shift = max(n_ent - 1, 1).bit_length()
    assert (N_EXP + 1) << shift < 2 ** 31
    order = jnp.sort((keys << shift) | jnp.arange(n_ent, dtype=I32)) & ((1 << shift) - 1)
    ids = jnp.where(order < n_asg, order, 0)
    n_tiles = n_asg // tm + N_EXP
    tile_end = jnp.cumsum(ptiles)
    t = jnp.arange(n_tiles, dtype=I32)
    te = jnp.sum((t[:, None] >= tile_end[None, :]).astype(I32), axis=1)
    used = te < N_EXP
    last_e = jnp.max(jnp.where(counts > 0, jnp.arange(N_EXP, dtype=I32), 0))
    te_c = jnp.where(used, te, last_e).astype(I32)
    start = jnp.sum(jnp.where(t[:, None] >= tile_end[None, :], ptiles[None, :], 0), axis=1)
    left = counts[jnp.minimum(te, N_EXP - 1)] - (t - start) * tm
    tv = jnp.where(used, jnp.clip(left, 0, tm), 0).astype(I32)
    return ids.reshape(n_tiles, 1, tm), te_c, tv


def _rope_tables(seq):
    rows = seq // GRID_W
    row = np.repeat(np.arange(rows, dtype=np.float64), GRID_W)
    col = np.tile(np.arange(GRID_W, dtype=np.float64), rows)
    n_freq = MLA_ROPE // 4
    inv = ROPE_BASE ** (-np.arange(n_freq, dtype=np.float64) / n_freq)
    ang = np.concatenate([row[:, None] * inv, col[:, None] * inv], axis=-1)
    cos, sin = np.cos(ang), np.sin(ang)
    cos128 = np.tile(cos, (1, 4)).astype(np.float32)
    sin128 = np.tile(sin, (1, 4)).astype(np.float32)
    sin_signed = np.tile(np.concatenate([-sin, sin], axis=-1), (1, 2)).astype(np.float32)
    return jnp.asarray(cos128), jnp.asarray(sin128), jnp.asarray(sin_signed)


def _dft_tables(seq):
    k = np.arange(seq, dtype=np.int64)
    ang = 2.0 * np.pi * ((k[:, None] * k[None, :]) % seq).astype(np.float64) / seq
    ct = (np.cos(ang) / np.sqrt(seq)).astype(np.float32)
    st = (np.sin(ang) / np.sqrt(seq)).astype(np.float32)
    c = np.arange(FNET_GD, dtype=np.int64)
    angc = 2.0 * np.pi * ((c[:, None] * c[None, :]) % FNET_GD).astype(np.float64) / FNET_GD
    eye = np.eye(FNET_G)
    cc = np.kron(eye, np.cos(angc) / np.sqrt(FNET_GD)).astype(np.float32)
    sc = np.kron(eye, np.sin(angc) / np.sqrt(FNET_GD)).astype(np.float32)
    return tuple(jnp.asarray(t).astype(BF16) for t in (ct, st, cc, sc))


def _rot_half_cols(w):
    return jnp.concatenate([-w[..., MLA_ROPE // 2:], w[..., :MLA_ROPE // 2]], axis=-1)


def _even_weights(w_in, w_uq, w_ukv):
    o = 3 * SC_W + MLA_QR + MLA_KVR
    kpe = w_in[:, o:o + MLA_ROPE]
    z64 = jnp.zeros((D, LANE - MLA_ROPE), w_in.dtype)
    w_in_ext = jnp.concatenate([w_in[:, :o], kpe, z64, _rot_half_cols(kpe), z64], axis=1).astype(BF16)
    wq = w_uq.reshape(MLA_QR, MLA_H, MLA_NOPE + MLA_ROPE)
    zq = jnp.zeros((MLA_QR, MLA_H, LANE - MLA_ROPE), w_uq.dtype)
    wq_main = jnp.concatenate([wq, zq], axis=-1).reshape(MLA_QR, MLA_H * 256).astype(BF16)
    wq_swap = jnp.concatenate([_rot_half_cols(wq[..., MLA_NOPE:]), zq], axis=-1).reshape(MLA_QR, MLA_H * LANE)
    return w_in_ext, wq_main, wq_swap.astype(BF16), w_ukv.astype(BF16)


def kernel(x, c, ctx, c_ctx, ev_mod_w, ev_mod_b, ev_norm_g, ev_w_in, ev_conv_w, ev_q_norm_g, ev_w_uq, ev_kv_norm_g,
           ev_w_ukv, ev_w_out, ev_ffn_gate, ev_ffn_up, ev_ffn_down, od_mod_w, od_mod_b, od_norm_g, od_w_in,
           od_lambda, od_subln_g, od_w_out, od_router, od_exp_gate, od_exp_up, od_exp_down):
    nb, seq, _ = x.shape
    ctx_len = ctx.shape[1]
    n = nb * seq
    nc = nb * ctx_len
    assert seq % GRID_W == 0 and seq % 128 == 0 and ctx_len % 128 == 0
    mod_rows = ((nb + 1 + 7) // 8) * 8
    cond = jnp.zeros((mod_rows, D), F32).at[:nb].set(c).at[nb].set(c_ctx)
    lat_row = lambda b: b
    ctx_row = lambda b: nb

    cos128, sin128, sin_signed = _rope_tables(seq)
    ones_c = jnp.ones((ctx_len, LANE), F32)
    zeros_c = jnp.zeros((ctx_len, LANE), F32)
    x2d = x.reshape(n, D)
    c2d = ctx.reshape(nc, D)

    mods = _modulation(cond, ev_mod_w[0].astype(BF16), ev_mod_b[0])
    sh1, sc1, g1, sh2, sc2, g2 = [m.reshape(mod_rows, 1, D) for m in jnp.split(mods, N_MOD, axis=-1)]
    ng = ev_norm_g[0].reshape(4, 1, D)
    w_in_e, wq_main, wq_swap, wkv = _even_weights(ev_w_in[0], ev_w_uq[0], ev_w_ukv[0])
    qg = ev_q_norm_g[0].reshape(1, MLA_QR)
    kvg = ev_kv_norm_g[0].reshape(1, MLA_KVR)
    w_out_e = ev_w_out[0].astype(BF16)
    wg_e, wu_e, wd_e = ev_ffn_gate[0].astype(BF16), ev_ffn_up[0].astype(BF16), ev_ffn_down[0].astype(BF16)

    bg_l, uc_l, q_l, k_l, v_l = _front_even(x2d, sh1, sc1, lat_row, ng[0], w_in_e, qg, wq_main, wq_swap, kvg, wkv,
                                            cos128, sin128, seq)
    bg_c, uc_c, q_c, k_c, v_c = _front_even(c2d, sh1, sc1, ctx_row, ng[0], w_in_e, qg, wq_main, wq_swap, kvg, wkv,
                                            ones_c, zeros_c, ctx_len)
    at_l = _mla_attn(q_l, k_c, v_c, k_l, v_l, nb, seq, ctx_len)
    at_c = _mla_attn(q_c, k_c, v_c, None, None, nb, ctx_len, ctx_len)
    x1_l, h2_l = _mix_even(bg_l, uc_l, ev_conv_w[0], at_l, w_out_e, x2d, g1, ng[1], ng[2], sh2, sc2, lat_row, seq)
    x1_c, h2_c = _mix_even(bg_c, uc_c, ev_conv_w[0], at_c, w_out_e, c2d, g1, ng[1], ng[2], sh2, sc2, ctx_row,
                           ctx_len)
    x2d = _ffn_dense(h2_l, wg_e, wu_e, wd_e, x1_l, g2, ng[3], lat_row, seq)
    c2d = _ffn_dense(h2_c, wg_e, wu_e, wd_e, x1_c, g2, ng[3], ctx_row, ctx_len)

    lam_init = 0.8 - 0.6 * math.exp(-0.3 * 1)
    mods = _modulation(cond, od_mod_w[0].astype(BF16), od_mod_b[0])
    sh1, sc1, g1, sh2, sc2, g2 = [m.reshape(mod_rows, 1, D) for m in jnp.split(mods, N_MOD, axis=-1)]
    ng = od_norm_g[0].reshape(4, 1, D)
    w_in_o = od_w_in[0].astype(BF16)
    q_o, k_o, v_o, f_o = _front_odd(x2d, sh1, sc1, lat_row, ng[0], w_in_o, cos128, sin_signed, seq, True)
    kc_o, vc_o = _front_odd(c2d, sh1, sc1, ctx_row, ng[0], w_in_o[:, DIFF_W:3 * DIFF_W], ones_c, zeros_c, ctx_len,
                            False)
    ca = _diff_attn(q_o, kc_o, vc_o, k_o, v_o, od_lambda[0], od_subln_g[0].reshape(1, 2 * DIFF_HD), nb, seq,
                    ctx_len, lam_init)
    ct, st, cc, sc = _dft_tables(seq)
    fd = _fourier(f_o, ct, st, cc, sc, nb, seq)
    router_f = jnp.zeros((D, LANE), F32).at[:, :N_EXP].set(od_router[0])
    router_hi = router_f.astype(BF16)
    router_pad = jnp.concatenate([router_hi, (router_f - router_hi.astype(F32)).astype(BF16)], axis=1)
    x1, h2p, ridx, rwt = _mix_odd(ca, fd, od_w_out[0].astype(BF16), x2d, g1, ng[1], ng[2], sh2, sc2, router_pad,
                                  lat_row, seq)
    ids, tile_expert, tile_valid = _route(ridx, n, TM_EXP)
    ys = _moe_experts(tile_expert, tile_valid, ids, h2p, od_exp_gate[0].astype(BF16), od_exp_up[0].astype(BF16),
                      od_exp_down[0].astype(BF16))
    out = _moe_combine(ys, rwt, x1, g2, ng[3], lat_row, seq)
    return out.reshape(nb, seq, D)
```

```python
import functools
import math

import numpy as np
import jax
import jax.numpy as jnp
from jax import lax
from jax.experimental import pallas as pl
from jax.experimental.pallas import tpu as pltpu

F32 = jnp.float32
BF16 = jnp.bfloat16
I32 = jnp.int32

D = 1024
GRID_W = 64
EPS = 1e-6
ROPE_BASE = 10000.0
N_MOD = 6
SC_W = D // 2
MLA_V = 128
MLA_NOPE = 128
MLA_ROPE = 64
MLA_H = (D - SC_W) // MLA_V
MLA_QR = 3 * D // 8
MLA_KVR = D // 4
MLA_SCALE = (MLA_NOPE + MLA_ROPE) ** -0.5
DIFF_W = 3 * D // 4
DIFF_HD = 64
DIFF_H = DIFF_W // (2 * DIFF_HD)
DIFF_SCALE = DIFF_HD ** -0.5
FNET_W = D - DIFF_W
FNET_G = 4
FNET_GD = FNET_W // FNET_G
D_FF = ((8 * D // 3 + 127) // 128) * 128
N_EXP = 8
TOP_K = 2
LOG2E = math.log2(math.e)

LANE = 128
VMEM_LIMIT = 56 * 1024 * 1024

TM = 1024
TM_FFN = 512
N_ROW_SPLIT = 4
TQ_MLA = 512
HP_MLA = 4
TQ_DIFF = 256
BK_ATTN = 256
HP_DIFF = 3
TK_FFT = 512
TM_EXP = 512
N_DMA_GROUPS = 2


def _cp(sem, vmem=VMEM_LIMIT):
    return pltpu.CompilerParams(dimension_semantics=sem, vmem_limit_bytes=vmem)


def _rms(x, g):
    return x * lax.rsqrt(jnp.mean(x * x, axis=-1, keepdims=True) + EPS) * g


def _dot(a, b):
    return jnp.dot(a, b, preferred_element_type=F32)


def _dot_nt(a, b):
    return lax.dot_general(a, b, (((1,), (1,)), ((), ())), preferred_element_type=F32)


def _full(shape):
    nd = len(shape)
    return pl.BlockSpec(shape, lambda *_: (0,) * nd)


def _mod_kernel(c_ref, w_ref, b_ref, o_ref):
    c = c_ref[...]
    s = c / (1.0 + jnp.exp(-c))
    o_ref[...] = _dot(s.astype(BF16), w_ref[...]) + b_ref[...]


def _modulation(cond, w_bf, b):
    rows = cond.shape[0]
    n = w_bf.shape[1]
    tn = 1536
    return pl.pallas_call(
        _mod_kernel,
        grid=(n // tn,),
        in_specs=[_full((rows, D)), pl.BlockSpec((D, tn), lambda j: (0, j)), pl.BlockSpec((1, tn), lambda j: (0, j))],
        out_specs=pl.BlockSpec((rows, tn), lambda j: (0, j)),
        out_shape=jax.ShapeDtypeStruct((rows, n), F32),
        compiler_params=_cp(("arbitrary",)),
        name="modulation",
    )(cond, w_bf, b.reshape(1, n))


def _front_even_kernel(x_ref, sh_ref, sc_ref, g0_ref, win_ref, qg_ref, wq_ref, wqs_ref, kvg_ref, wkv_ref,
                       cos_ref, sin_ref, bg_ref, uc_ref, q_ref, k_ref, v_ref):
    x = x_ref[...]
    h = _rms(x, g0_ref[...] * (1.0 + sc_ref[0])) + sh_ref[0]
    z = _dot(h.astype(BF16), win_ref[...])
    bg_ref[...] = z[:, 0:SC_W].astype(BF16)
    uc_ref[...] = (z[:, SC_W:2 * SC_W] * z[:, 2 * SC_W:3 * SC_W]).astype(BF16)
    o = 3 * SC_W
    zq = z[:, o:o + MLA_QR]
    zkv = z[:, o + MLA_QR:o + MLA_QR + MLA_KVR]
    o2 = o + MLA_QR + MLA_KVR
    kpe = z[:, o2:o2 + LANE]
    kpes = z[:, o2 + LANE:o2 + 2 * LANE]
    cos = cos_ref[...]
    sin = sin_ref[...]
    zqn = _rms(zq, qg_ref[...]).astype(BF16)
    qm = _dot(zqn, wq_ref[...])
    qs = _dot(zqn, wqs_ref[...])
    qscale = MLA_SCALE * LOG2E
    for hd in range(MLA_H):
        lo = qm[:, 256 * hd:256 * hd + LANE]
        hi = qm[:, 256 * hd + LANE:256 * hd + 2 * LANE] * cos + qs[:, LANE * hd:LANE * hd + LANE] * sin
        q_ref[hd, :, 0:LANE] = (lo * qscale).astype(BF16)
        q_ref[hd, :, LANE:2 * LANE] = (hi * qscale).astype(BF16)
    zkvn = _rms(zkv, kvg_ref[...]).astype(BF16)
    kv = _dot(zkvn, wkv_ref[...])
    kpr = (kpe * cos + kpes * sin).astype(BF16)
    for hd in range(MLA_H):
        k_ref[hd, :, 0:LANE] = kv[:, 256 * hd:256 * hd + LANE].astype(BF16)
        k_ref[hd, :, LANE:2 * LANE] = kpr
        v_ref[hd] = kv[:, 256 * hd + LANE:256 * hd + 2 * LANE].astype(BF16)


def _front_even(x2d, sh, sc, row_of, g0, w_in, qg, wq, wqs, kvg, wkv, cos, sin, seq):
    n = x2d.shape[0]
    tm = min(TM, seq)
    nper = seq // tm
    rowspec = pl.BlockSpec((1, 1, D), lambda i: (row_of(i // nper), 0, 0))
    tabspec = pl.BlockSpec((tm, LANE), lambda i: (i % nper, 0))
    win_n = w_in.shape[1]
    return pl.pallas_call(
        _front_even_kernel,
        grid=(n // tm,),
        in_specs=[pl.BlockSpec((tm, D), lambda i: (i, 0)), rowspec, rowspec, _full((1, D)), _full((D, win_n)),
                  _full((1, MLA_QR)), _full((MLA_QR, 4 * 256)), _full((MLA_QR, 4 * LANE)),
                  _full((1, MLA_KVR)), _full((MLA_KVR, 4 * 256)), tabspec, tabspec],
        out_specs=[pl.BlockSpec((tm, SC_W), lambda i: (i, 0)), pl.BlockSpec((tm, SC_W), lambda i: (i, 0)),
                   pl.BlockSpec((MLA_H, tm, 256), lambda i: (0, i, 0)),
                   pl.BlockSpec((MLA_H, tm, 256), lambda i: (0, i, 0)),
                   pl.BlockSpec((MLA_H, tm, LANE), lambda i: (0, i, 0))],
        out_shape=[jax.ShapeDtypeStruct((n, SC_W), BF16), jax.ShapeDtypeStruct((n, SC_W), BF16),
                   jax.ShapeDtypeStruct((MLA_H, n, 256), BF16), jax.ShapeDtypeStruct((MLA_H, n, 256), BF16),
                   jax.ShapeDtypeStruct((MLA_H, n, LANE), BF16)],
        compiler_params=_cp(("parallel",)),
        name="front_even",
    )(x2d, sh, sc, g0, w_in, qg, wq, wqs, kvg, wkv, cos, sin)


def _key_blocks(k_ref, v_ref, bk):
    n = k_ref.shape[0]
    return [(k_ref.at[pl.ds(j, min(bk, n - j))], v_ref.at[pl.ds(j, min(bk, n - j))]) for j in range(0, n, bk)]


def _online_softmax_pv(q, blocks):
    m = acc = None
    dv = blocks[0][1].shape[1]
    for k_blk, v_blk in blocks:
        bk = k_blk.shape[0]
        ones_col = (lax.broadcasted_iota(I32, (bk, LANE), 1) == 0).astype(BF16)
        v_ext = jnp.concatenate([v_blk[...], ones_col], axis=1)
        s = _dot_nt(q, k_blk[...])
        mb = jnp.max(s, axis=-1, keepdims=True)
        if m is None:
            m = mb
            acc = _dot(jnp.exp2(s - m).astype(BF16), v_ext)
        else:
            m_new = jnp.maximum(m, mb)
            acc = jnp.exp2(m - m_new) * acc + _dot(jnp.exp2(s - m_new).astype(BF16), v_ext)
            m = m_new
    return acc[:, :dv], acc[:, dv:dv + 1]


def _mla_attn_kernel(*refs, with_lat, bk):
    if with_lat:
        q_ref, kc_ref, vc_ref, kl_ref, vl_ref, o_ref = refs
    else:
        q_ref, kc_ref, vc_ref, o_ref = refs
    for hd in range(q_ref.shape[0]):
        blocks = _key_blocks(kc_ref.at[hd], vc_ref.at[hd], bk)
        if with_lat:
            blocks += _key_blocks(kl_ref.at[hd], vl_ref.at[hd], bk)
        acc, l = _online_softmax_pv(q_ref[hd], blocks)
        o_ref[:, hd * LANE:(hd + 1) * LANE] = (acc * (1.0 / l)).astype(BF16)


def _mla_attn(q, k_ctx, v_ctx, k_lat, v_lat, nb, seq_q, ctx_len):
    with_lat = k_lat is not None
    tq = min(TQ_MLA, seq_q)
    nq = seq_q // tq
    hp = HP_MLA
    in_specs = [pl.BlockSpec((hp, tq, 256), lambda b, h, i: (h, b * nq + i, 0)),
                pl.BlockSpec((hp, ctx_len, 256), lambda b, h, i: (h, b, 0)),
                pl.BlockSpec((hp, ctx_len, LANE), lambda b, h, i: (h, b, 0))]
    args = [q, k_ctx, v_ctx]
    if with_lat:
        in_specs += [pl.BlockSpec((hp, seq_q, 256), lambda b, h, i: (h, b, 0)),
                     pl.BlockSpec((hp, seq_q, LANE), lambda b, h, i: (h, b, 0))]
        args += [k_lat, v_lat]
    return pl.pallas_call(
        functools.partial(_mla_attn_kernel, with_lat=with_lat, bk=BK_ATTN),
        grid=(nb, MLA_H // hp, nq),
        in_specs=in_specs,
        out_specs=pl.BlockSpec((tq, hp * LANE), lambda b, h, i: (b * nq + i, h)),
        out_shape=jax.ShapeDtypeStruct((nb * seq_q, MLA_H * MLA_V), BF16),
        compiler_params=_cp(("parallel", "parallel", "arbitrary")),
        name="mla_attn_lat" if with_lat else "mla_attn_ctx",
    )(*args)


def _residual_and_h2(y, x, g1_ref, n1_ref, n2_ref, sh2_ref, sc2_ref):
    x1 = x + _rms(y, g1_ref[0] * n1_ref[...])
    h2 = _rms(x1, n2_ref[...] * (1.0 + sc2_ref[0])) + sh2_ref[0]
    return x1, h2


def _mix_even_kernel(bg_ref, uc_ref, ucp_ref, ucn_ref, cw_ref, at_ref, wo_ref, x_ref, g1_ref, n1_ref, n2_ref,
                     sh2_ref, sc2_ref, x1_ref, h2_ref, scr, *, nper):
    tm = uc_ref.shape[0]
    i = pl.program_id(0)
    ucf = uc_ref[...].astype(F32)
    first = (i % nper) == 0
    last = (i % nper) == nper - 1
    prev_row = jnp.where(first, 0.0, ucp_ref[7:8, :].astype(F32))
    next_row = jnp.where(last, 0.0, ucn_ref[0:1, :].astype(F32))
    scr[8:8 + tm, :] = ucf
    scr[7:8, :] = prev_row
    scr[8 + tm:9 + tm, :] = next_row
    up = scr[7:7 + tm, :]
    dn = scr[9:9 + tm, :]
    conv = cw_ref[0:1, :] * up + cw_ref[1:2, :] * ucf + cw_ref[2:3, :] * dn
    a = (bg_ref[...].astype(F32) * conv).astype(BF16)
    rows = tm // N_ROW_SPLIT
    for p in range(N_ROW_SPLIT):
        rs = slice(p * rows, (p + 1) * rows)
        y = _dot(a[rs], wo_ref[0:SC_W, :]) + _dot(at_ref[rs, :], wo_ref[SC_W:D, :])
        x1, h2 = _residual_and_h2(y, x_ref[rs, :], g1_ref, n1_ref, n2_ref, sh2_ref, sc2_ref)
        x1_ref[rs, :] = x1
        h2_ref[rs, :] = h2.astype(BF16)


def _mix_even(bg, uc, conv_w, attn, w_out, x2d, g1, n1, n2, sh2, sc2, row_of, seq):
    n = x2d.shape[0]
    tm = min(TM, seq)
    nper = seq // tm
    nb8 = n // 8
    rowspec = pl.BlockSpec((1, 1, D), lambda i: (row_of(i // nper), 0, 0))
    tile = lambda w: pl.BlockSpec((tm, w), lambda i: (i, 0))
    return pl.pallas_call(
        functools.partial(_mix_even_kernel, nper=nper),
        grid=(n // tm,),
        in_specs=[tile(SC_W), tile(SC_W),
                  pl.BlockSpec((8, SC_W), lambda i: (jnp.maximum(i * (tm // 8) - 1, 0), 0)),
                  pl.BlockSpec((8, SC_W), lambda i: (jnp.minimum((i + 1) * (tm // 8), nb8 - 1), 0)),
                  _full((3, SC_W)), tile(MLA_H * MLA_V), _full((D, D)), tile(D), rowspec, _full((1, D)),
                  _full((1, D)), rowspec, rowspec],
        out_specs=[tile(D), tile(D)],
        out_shape=[jax.ShapeDtypeStruct((n, D), F32), jax.ShapeDtypeStruct((n, D), BF16)],
        scratch_shapes=[pltpu.VMEM((tm + 16, SC_W), F32)],
        compiler_params=_cp(("parallel",)),
        name="mix_even",
    )(bg, uc, uc, uc, conv_w, attn, w_out, x2d, g1, n1, n2, sh2, sc2)


def _mix_odd_kernel(ca_ref, fd_ref, wo_ref, x_ref, g1_ref, n1_ref, n2_ref, sh2_ref, sc2_ref, rw_ref,
                    x1_ref, h2p_ref, ridx_ref, rwt_ref):
    rows = x_ref.shape[0] // N_ROW_SPLIT
    h2s = []
    for p in range(N_ROW_SPLIT):
        rs = slice(p * rows, (p + 1) * rows)
        y = _dot(ca_ref[rs, :], wo_ref[0:DIFF_W, :]) + _dot(fd_ref[rs, :], wo_ref[DIFF_W:D, :])
        x1, h2g = _residual_and_h2(y, x_ref[rs, :], g1_ref, n1_ref, n2_ref, sh2_ref, sc2_ref)
        x1_ref[rs, :] = x1
        h2p_ref[rs, :] = h2g
        h2s.append(h2g)
    h2 = jnp.concatenate(h2s, axis=0)
    tm = h2.shape[0]
    hi = h2.astype(BF16)
    lo = (h2 - hi.astype(F32)).astype(BF16)
    r = _dot(jnp.concatenate([hi, lo], axis=0), rw_ref[...])
    logits = (r[:tm, :LANE] + r[:tm, LANE:]) + (r[tm:, :LANE] + r[tm:, LANE:])
    lane = lax.broadcasted_iota(I32, logits.shape, 1).astype(F32)
    neg = jnp.float32(-jnp.inf)
    s0 = jnp.where(lane < N_EXP, logits, neg)
    m1 = jnp.max(s0, axis=-1, keepdims=True)
    i1 = jnp.min(jnp.where(s0 == m1, lane, float(LANE)), axis=-1, keepdims=True)
    s1 = jnp.where(lane == i1, neg, s0)
    m2 = jnp.max(s1, axis=-1, keepdims=True)
    i2 = jnp.min(jnp.where(s1 == m2, lane, float(LANE)), axis=-1, keepdims=True)
    e = jnp.exp(m2 - m1)
    w1 = 1.0 / (1.0 + e)
    w2 = e * w1
    ridx_ref[...] = jnp.where(lane == 0.0, i1, jnp.where(lane == 1.0, i2, 0.0)).astype(I32)
    rwt_ref[...] = jnp.where(lane == 0.0, w1, jnp.where(lane == 1.0, w2, 0.0))


def _mix_odd(cattn, fd, w_out, x2d, g1, n1, n2, sh2, sc2, router_pad, row_of, seq):
    n = x2d.shape[0]
    tm = min(TM, seq)
    nper = seq // tm
    rowspec = pl.BlockSpec((1, 1, D), lambda i: (row_of(i // nper), 0, 0))
    tile = lambda w: pl.BlockSpec((tm, w), lambda i: (i, 0))
    return pl.pallas_call(
        _mix_odd_kernel,
        grid=(n // tm,),
        in_specs=[tile(DIFF_W), tile(FNET_W), _full((D, D)), tile(D), rowspec, _full((1, D)), _full((1, D)),
                  rowspec, rowspec, _full((D, 2 * LANE))],
        out_specs=[tile(D), tile(D), tile(LANE), tile(LANE)],
        out_shape=[jax.ShapeDtypeStruct((n, D), F32), jax.ShapeDtypeStruct((n, D), F32),
                   jax.ShapeDtypeStruct((n, LANE), I32), jax.ShapeDtypeStruct((n, LANE), F32)],
        compiler_params=_cp(("parallel",)),
        name="mix_odd",
    )(cattn, fd, w_out, x2d, g1, n1, n2, sh2, sc2, router_pad)


def _swiglu(h, wg, wu, wd):
    g = _dot(h, wg)
    u = _dot(h, wu)
    a = (g / (1.0 + jnp.exp(-g)) * u).astype(BF16)
    return _dot(a, wd)


def _ffn_kernel(h_ref, wg_ref, wu_ref, wd_ref, x_ref, g2_ref, n3_ref, o_ref):
    f = _swiglu(h_ref[...], wg_ref[...], wu_ref[...], wd_ref[...])
    o_ref[...] = x_ref[...] + g2_ref[0] * _rms(f, n3_ref[...])


def _ffn_dense(h2, wg, wu, wd, x1, g2, n3, row_of, seq):
    n = x1.shape[0]
    tm = min(TM_FFN, seq)
    nper = seq // tm
    rowspec = pl.BlockSpec((1, 1, D), lambda i: (row_of(i // nper), 0, 0))
    tile = lambda w: pl.BlockSpec((tm, w), lambda i: (i, 0))
    once = lambda shape: pl.BlockSpec(shape, lambda i: (0, 0), pipeline_mode=pl.Buffered(1))
    return pl.pallas_call(
        _ffn_kernel,
        grid=(n // tm,),
        in_specs=[tile(D), once((D, D_FF)), once((D, D_FF)), once((D_FF, D)), tile(D), rowspec, _full((1, D))],
        out_specs=tile(D),
        out_shape=jax.ShapeDtypeStruct((n, D), F32),
        compiler_params=_cp(("parallel",)),
        name="ffn_dense",
    )(h2, wg, wu, wd, x1, g2, n3)


def _rope_slab(x, cos, sin_signed, lane):
    swap = jnp.where((lane & 63) < 32, pltpu.roll(x, 96, 1), pltpu.roll(x, 32, 1))
    return x * cos + swap * sin_signed


def _front_odd_kernel(x_ref, sh_ref, sc_ref, g0_ref, win_ref, cos_ref, sin_ref, q_ref, k_ref, v_ref, f_ref,
                      *, with_q):
    x = x_ref[...]
    h = _rms(x, g0_ref[...] * (1.0 + sc_ref[0])) + sh_ref[0]
    z = _dot(h.astype(BF16), win_ref[...])
    cos = cos_ref[...]
    sin = sin_ref[...]
    lane = lax.broadcasted_iota(I32, cos.shape, 1)
    off = DIFF_W if with_q else 0
    qscale = DIFF_SCALE * LOG2E
    for g in range(DIFF_W // LANE):
        sl = slice(LANE * g, LANE * g + LANE)
        if with_q:
            q_ref[:, sl] = (_rope_slab(z[:, sl], cos, sin, lane) * qscale).astype(BF16)
        ksl = slice(off + LANE * g, off + LANE * g + LANE)
        k_ref[:, sl] = _rope_slab(z[:, ksl], cos, sin, lane).astype(BF16)
    v_ref[...] = z[:, off + DIFF_W:off + 2 * DIFF_W].astype(BF16)
    if with_q:
        f_ref[...] = z[:, 3 * DIFF_W:3 * DIFF_W + FNET_W].astype(BF16)


def _front_odd(x2d, sh, sc, row_of, g0, w_in, cos, sin, seq, with_q):
    n = x2d.shape[0]
    tm = min(TM, seq)
    nper = seq // tm
    rowspec = pl.BlockSpec((1, 1, D), lambda i: (row_of(i // nper), 0, 0))
    tabspec = pl.BlockSpec((tm, LANE), lambda i: (i % nper, 0))
    tile = lambda w: pl.BlockSpec((tm, w), lambda i: (i, 0))
    if with_q:
        kern = functools.partial(_front_odd_kernel, with_q=True)
        out_specs = [tile(DIFF_W), tile(DIFF_W), tile(DIFF_W), tile(FNET_W)]
        out_shape = [jax.ShapeDtypeStruct((n, DIFF_W), BF16)] * 3 + [jax.ShapeDtypeStruct((n, FNET_W), BF16)]
    else:
        def kern(x_ref, sh_ref, sc_ref, g0_ref, win_ref, cos_ref, sin_ref, k_ref, v_ref):
            _front_odd_kernel(x_ref, sh_ref, sc_ref, g0_ref, win_ref, cos_ref, sin_ref, None, k_ref, v_ref, None,
                              with_q=False)
        out_specs = [tile(DIFF_W), tile(DIFF_W)]
        out_shape = [jax.ShapeDtypeStruct((n, DIFF_W), BF16)] * 2
    return pl.pallas_call(
        kern,
        grid=(n // tm,),
        in_specs=[tile(D), rowspec, rowspec, _full((1, D)), _full((D, w_in.shape[1])), tabspec, tabspec],
        out_specs=out_specs,
        out_shape=out_shape,
        compiler_params=_cp(("parallel",)),
        name="front_odd" if with_q else "front_odd_ctx",
    )(x2d, sh, sc, g0, w_in, cos, sin)


def _diff_attn_kernel(q_ref, kc_ref, vc_ref, kl_ref, vl_ref, lam_ref, sg_ref, o_ref, *, lam_init, bk):
    tq = q_ref.shape[0]
    lp = lam_ref[...]
    lam = (jnp.exp(jnp.sum(lp[0:1, :] * lp[1:2, :], axis=-1, keepdims=True))
           - jnp.exp(jnp.sum(lp[2:3, :] * lp[3:4, :], axis=-1, keepdims=True)) + lam_init)
    lane = lax.broadcasted_iota(I32, (tq, LANE), 1)
    for hd in range(q_ref.shape[1] // LANE):
        sl = pl.ds(hd * LANE, LANE)
        q = q_ref[:, sl]
        zero = jnp.zeros_like(q)
        qq = jnp.concatenate([jnp.where(lane < DIFF_HD, q, zero), jnp.where(lane >= DIFF_HD, q, zero)], axis=0)
        blocks = (_key_blocks(kc_ref.at[:, sl], vc_ref.at[:, sl], bk)
                  + _key_blocks(kl_ref.at[:, sl], vl_ref.at[:, sl], bk))
        acc, l = _online_softmax_pv(qq, blocks)
        r = 1.0 / l
        o = acc[:tq] * r[:tq] - acc[tq:] * (r[tq:] * lam)
        o_ref[:, sl] = (_rms(o, sg_ref[...]) * (1.0 - lam_init)).astype(BF16)


def _diff_attn(q, k_ctx, v_ctx, k_lat, v_lat, lam_p, subln_g, nb, seq, ctx_len, lam_init):
    tq = min(TQ_DIFF, seq)
    nq = seq // tq
    return pl.pallas_call(
        functools.partial(_diff_attn_kernel, lam_init=lam_init, bk=BK_ATTN),
        grid=(nb, DIFF_H // HP_DIFF, nq),
        in_specs=[pl.BlockSpec((tq, HP_DIFF * LANE), lambda b, h, i: (b * nq + i, h)),
                  pl.BlockSpec((ctx_len, HP_DIFF * LANE), lambda b, h, i: (b, h)),
                  pl.BlockSpec((ctx_len, HP_DIFF * LANE), lambda b, h, i: (b, h)),
                  pl.BlockSpec((seq, HP_DIFF * LANE), lambda b, h, i: (b, h)),
                  pl.BlockSpec((seq, HP_DIFF * LANE), lambda b, h, i: (b, h)),
                  _full((4, DIFF_HD)), _full((1, 2 * DIFF_HD))],
        out_specs=pl.BlockSpec((tq, HP_DIFF * LANE), lambda b, h, i: (b * nq + i, h)),
        out_shape=jax.ShapeDtypeStruct((nb * seq, DIFF_W), BF16),
        compiler_params=_cp(("parallel", "parallel", "arbitrary")),
        name="diff_attn",
    )(q, k_ctx, v_ctx, k_lat, v_lat, lam_p, subln_g)


def _fourier_kernel(ct_ref, st_ref, f_ref, cc_ref, sc_ref, o_ref):
    f = f_ref[...]
    p = _dot(ct_ref[...], f).astype(BF16)
    q = _dot(st_ref[...], f).astype(BF16)
    o_ref[...] = (_dot(p, cc_ref[...]) - _dot(q, sc_ref[...])).astype(BF16)


def _fourier(f2d, ct, st, cc, sc, nb, seq):
    tk = min(TK_FFT, seq)
    nk = seq // tk
    return pl.pallas_call(
        _fourier_kernel,
        grid=(nk, nb),
        in_specs=[pl.BlockSpec((tk, seq), lambda j, b: (j, 0)), pl.BlockSpec((tk, seq), lambda j, b: (j, 0)),
                  pl.BlockSpec((seq, FNET_W), lambda j, b: (b, 0)), _full((FNET_W, FNET_W)), _full((FNET_W, FNET_W))],
        out_specs=pl.BlockSpec((tk, FNET_W), lambda j, b: (b * nk + j, 0)),
        out_shape=jax.ShapeDtypeStruct((nb * seq, FNET_W), BF16),
        compiler_params=_cp(("arbitrary", "arbitrary")),
        name="fourier",
    )(ct, st, f2d, cc, sc)


def _swiglu_chunks(x, wg_ref, wu_ref, wd_ref, n_chunks, between):
    f = wg_ref.shape[2]
    step = -(-f // (n_chunks * 256)) * 256
    acc = None
    for c in range(n_chunks):
        lo, hi = c * step, min((c + 1) * step, f)
        g = _dot(x, wg_ref[0, :, lo:hi])
        u = _dot(x, wu_ref[0, :, lo:hi])
        a = (g / (1.0 + jnp.exp(-g)) * u).astype(BF16)
        part = _dot(a, wd_ref[0, lo:hi, :])
        acc = part if acc is None else acc + part
        between(c)
    return acc


def _moe_expert_kernel(te_ref, tv_ref, idx_ref, idxn_ref, idxp_ref, h_ref, wg_ref, wu_ref, wd_ref, out_ref,
                       xbuf, ybuf, gsem, ssem, *, n_tok, n_chunks):
    tm = xbuf.shape[1]
    i = pl.program_id(0)
    n = pl.num_programs(0)
    cur = i % 2
    nvalid = tv_ref[i]
    nprev = jnp.where(i >= 1, tv_ref[jnp.maximum(i - 1, 0)], 0)
    next_used = jnp.logical_and(i + 1 < n, tv_ref[jnp.minimum(i + 1, n - 1)] > 0)

    def gather_row(ids_ref, b, r):
        j = ids_ref[0, 0, r]
        tok = jnp.where(j >= n_tok, j - n_tok, j)
        return pltpu.make_async_copy(h_ref.at[pl.ds(tok, 1)], xbuf.at[b, pl.ds(r, 1)], gsem.at[b])

    def scatter_row(ids_ref, b, r):
        return pltpu.make_async_copy(ybuf.at[b, pl.ds(r, 1)], out_ref.at[pl.ds(ids_ref[0, 0, r], 1)], ssem.at[b])

    def scatter_loop(ids_ref, b, nv):
        def issue(r, c):
            @pl.when(r < nv)
            def _():
                scatter_row(ids_ref, b, r).start()
            return c
        lax.fori_loop(0, tm, issue, 0)

    def wait_scatter(k):
        nv = tv_ref[k]
        b = k % 2

        @pl.when(nv == tm)
        def _():
            pltpu.make_async_copy(ybuf.at[b], out_ref.at[pl.ds(0, tm)], ssem.at[b]).wait()

        @pl.when(jnp.logical_and(nv > 0, nv < tm))
        def _():
            def drain(r, c):
                @pl.when(r < nv)
                def _():
                    pltpu.make_async_copy(ybuf.at[b, pl.ds(r, 1)], out_ref.at[pl.ds(0, 1)], ssem.at[b]).wait()
                return c
            lax.fori_loop(0, tm, drain, 0, unroll=8)

    @pl.when(jnp.logical_and(i == 0, nvalid > 0))
    def _():
        def issue(r, c):
            gather_row(idx_ref, 0, r).start()
            return c
        lax.fori_loop(0, tm, issue, 0)

    @pl.when(i >= 2)
    def _():
        wait_scatter(i - 2)

    @pl.when(nvalid > 0)
    def _():
        pltpu.make_async_copy(h_ref.at[pl.ds(0, tm)], xbuf.at[cur], gsem.at[cur]).wait()
        x = xbuf[cur].astype(BF16)

        def between(c):
            groups = n_chunks - 1
            if c >= groups:
                return
            for r in range(c * tm // groups, (c + 1) * tm // groups):
                @pl.when(next_used)
                def _():
                    gather_row(idxn_ref, 1 - cur, r).start()

                @pl.when(r < nprev)
                def _():
                    scatter_row(idxp_ref, 1 - cur, r).start(priority=1)

        ybuf[cur] = _swiglu_chunks(x, wg_ref, wu_ref, wd_ref, n_chunks, between)

    @pl.when(jnp.logical_and(nvalid == 0, nprev > 0))
    def _():
        scatter_loop(idxp_ref, 1 - cur, nprev)

    @pl.when(i == n - 1)
    def _():
        @pl.when(nvalid > 0)
        def _():
            scatter_loop(idx_ref, cur, nvalid)

        @pl.when(i >= 1)
        def _():
            wait_scatter(i - 1)
        wait_scatter(i)


def _moe_experts(tile_expert, tile_valid, ids, h2p, wg, wu, wd):
    n_tiles, _, tm = ids.shape
    n_tok = h2p.shape[0]
    once = lambda shape: pl.BlockSpec(shape, lambda i, te, tv: (te[i], 0, 0), pipeline_mode=pl.Buffered(1))
    grid_spec = pltpu.PrefetchScalarGridSpec(
        num_scalar_prefetch=2,
        grid=(n_tiles,),
        in_specs=[pl.BlockSpec((1, 1, tm), lambda i, te, tv: (i, 0, 0), memory_space=pltpu.SMEM),
                  pl.BlockSpec((1, 1, tm), lambda i, te, tv: (jnp.minimum(i + 1, n_tiles - 1), 0, 0),
                               memory_space=pltpu.SMEM),
                  pl.BlockSpec((1, 1, tm), lambda i, te, tv: (jnp.maximum(i - 1, 0), 0, 0), memory_space=pltpu.SMEM),
                  pl.BlockSpec(memory_space=pl.ANY),
                  once((1, D, D_FF)), once((1, D, D_FF)), once((1, D_FF, D))],
        out_specs=pl.BlockSpec(memory_space=pl.ANY),
        scratch_shapes=[pltpu.VMEM((2, tm, D), F32), pltpu.VMEM((2, tm, D), F32),
                        pltpu.SemaphoreType.DMA((2,)), pltpu.SemaphoreType.DMA((2,))],
    )
    return pl.pallas_call(
        functools.partial(_moe_expert_kernel, n_tok=n_tok, n_chunks=N_DMA_GROUPS),
        grid_spec=grid_spec,
        out_shape=jax.ShapeDtypeStruct((TOP_K * n_tok, D), F32),
        compiler_params=_cp(("arbitrary",)),
        name="moe_experts",
    )(tile_expert, tile_valid, ids, ids, ids, h2p, wg, wu, wd)


def _moe_combine_kernel(y0_ref, y1_ref, rwt_ref, x_ref, g2_ref, n3_ref, o_ref):
    w = rwt_ref[...]
    f = y0_ref[...] * w[:, 0:1] + y1_ref[...] * w[:, 1:2]
    o_ref[...] = x_ref[...] + g2_ref[0] * _rms(f, n3_ref[...])


def _moe_combine(ys, rwt, x1, g2, n3, row_of, seq):
    n = x1.shape[0]
    ts = min(TM_FFN, seq)
    nper = seq // ts
    nsteps = n // ts
    rowspec = pl.BlockSpec((1, 1, D), lambda i: (row_of(i // nper), 0, 0))
    return pl.pallas_call(
        _moe_combine_kernel,
        grid=(nsteps,),
        in_specs=[pl.BlockSpec((ts, D), lambda i: (i, 0)), pl.BlockSpec((ts, D), lambda i: (i + nsteps, 0)),
                  pl.BlockSpec((ts, LANE), lambda i: (i, 0)), pl.BlockSpec((ts, D), lambda i: (i, 0)),
                  rowspec, _full((1, D))],
        out_specs=pl.BlockSpec((ts, D), lambda i: (i, 0)),
        out_shape=jax.ShapeDtypeStruct((n, D), F32),
        compiler_params=_cp(("arbitrary",)),
        name="moe_combine",
    )(ys, ys, rwt, x1, g2, n3)


def _route(ridx, n, tm):
    n_asg = TOP_K * n
    e_flat = jnp.concatenate([ridx[:, 0], ridx[:, 1]])
    counts = jnp.sum((e_flat[:, None] == jnp.arange(N_EXP, dtype=I32)[None, :]).astype(I32), axis=0)
    ptiles = (counts + tm - 1) // tm
    pad = ptiles * tm - counts
    fill_e = jnp.repeat(jnp.arange(N_EXP, dtype=I32), tm)
    fill_k = jnp.tile(jnp.arange(tm, dtype=I32), N_EXP)
    fill_key = jnp.where(fill_k < pad[fill_e], fill_e, N_EXP)
    keys = jnp.concatenate([e_flat, fill_key])
    n_ent = n_asg + N_EXP * tm
    shift = max(n_ent - 1, 1).bit_length()
    assert (N_EXP + 1) << shift < 2 ** 31
    order = jnp.sort((keys << shift) | jnp.arange(n_ent, dtype=I32)) & ((1 << shift) - 1)
    ids = jnp.where(order < n_asg, order, 0)
    n_tiles = n_asg // tm + N_EXP
    tile_end = jnp.cumsum(ptiles)
    t = jnp.arange(n_tiles, dtype=I32)
    te = jnp.sum((t[:, None] >= tile_end[None, :]).astype(I32), axis=1)
    used = te < N_EXP
    last_e = jnp.max(jnp.where(counts > 0, jnp.arange(N_EXP, dtype=I32), 0))
    te_c = jnp.where(used, te, last_e).astype(I32)
    start = jnp.sum(jnp.where(t[:, None] >= tile_end[None, :], ptiles[None, :], 0), axis=1)
    left = counts[jnp.minimum(te, N_EXP - 1)] - (t - start) * tm
    tv = jnp.where(used, jnp.clip(left, 0, tm), 0).astype(I32)
    return ids.reshape(n_tiles, 1, tm), te_c, tv


def _rope_tables(seq):
    rows = seq // GRID_W
    row = np.repeat(np.arange(rows, dtype=np.float64), GRID_W)
    col = np.tile(np.arange(GRID_W, dtype=np.float64), rows)
    n_freq = MLA_ROPE // 4
    inv = ROPE_BASE ** (-np.arange(n_freq, dtype=np.float64) / n_freq)
    ang = np.concatenate([row[:, None] * inv, col[:, None] * inv], axis=-1)
    cos, sin = np.cos(ang), np.sin(ang)
    cos128 = np.tile(cos, (1, 4)).astype(np.float32)
    sin128 = np.tile(sin, (1, 4)).astype(np.float32)
    sin_signed = np.tile(np.concatenate([-sin, sin], axis=-1), (1, 2)).astype(np.float32)
    return jnp.asarray(cos128), jnp.asarray(sin128), jnp.asarray(sin_signed)


def _dft_tables(seq):
    k = np.arange(seq, dtype=np.int64)
    ang = 2.0 * np.pi * ((k[:, None] * k[None, :]) % seq).astype(np.float64) / seq
    ct = (np.cos(ang) / np.sqrt(seq)).astype(np.float32)
    st = (np.sin(ang) / np.sqrt(seq)).astype(np.float32)
    c = np.arange(FNET_GD, dtype=np.int64)
    angc = 2.0 * np.pi * ((c[:, None] * c[None, :]) % FNET_GD).astype(np.float64) / FNET_GD
    eye = np.eye(FNET_G)
    cc = np.kron(eye, np.cos(angc) / np.sqrt(FNET_GD)).astype(np.float32)
    sc = np.kron(eye, np.sin(angc) / np.sqrt(FNET_GD)).astype(np.float32)
    return tuple(jnp.asarray(t).astype(BF16) for t in (ct, st, cc, sc))


def _rot_half_cols(w):
    return jnp.concatenate([-w[..., MLA_ROPE // 2:], w[..., :MLA_ROPE // 2]], axis=-1)


def _even_weights(w_in, w_uq, w_ukv):
    o = 3 * SC_W + MLA_QR + MLA_KVR
    kpe = w_in[:, o:o + MLA_ROPE]
    z64 = jnp.zeros((D, LANE - MLA_ROPE), w_in.dtype)
    w_in_ext = jnp.concatenate([w_in[:, :o], kpe, z64, _rot_half_cols(kpe), z64], axis=1).astype(BF16)
    wq = w_uq.reshape(MLA_QR, MLA_H, MLA_NOPE + MLA_ROPE)
    zq = jnp.zeros((MLA_QR, MLA_H, LANE - MLA_ROPE), w_uq.dtype)
    wq_main = jnp.concatenate([wq, zq], axis=-1).reshape(MLA_QR, MLA_H * 256).astype(BF16)
    wq_swap = jnp.concatenate([_rot_half_cols(wq[..., MLA_NOPE:]), zq], axis=-1).reshape(MLA_QR, MLA_H * LANE)
    return w_in_ext, wq_main, wq_swap.astype(BF16), w_ukv.astype(BF16)


def kernel(x, c, ctx, c_ctx, ev_mod_w, ev_mod_b, ev_norm_g, ev_w_in, ev_conv_w, ev_q_norm_g, ev_w_uq, ev_kv_norm_g,
           ev_w_ukv, ev_w_out, ev_ffn_gate, ev_ffn_up, ev_ffn_down, od_mod_w, od_mod_b, od_norm_g, od_w_in,
           od_lambda, od_subln_g, od_w_out, od_router, od_exp_gate, od_exp_up, od_exp_down):
    nb, seq, _ = x.shape
    ctx_len = ctx.shape[1]
    n = nb * seq
    nc = nb * ctx_len
    assert seq % GRID_W == 0 and seq % 128 == 0 and ctx_len % 128 == 0
    mod_rows = ((nb + 1 + 7) // 8) * 8
    cond = jnp.zeros((mod_rows, D), F32).at[:nb].set(c).at[nb].set(c_ctx)
    lat_row = lambda b: b
    ctx_row = lambda b: nb

    cos128, sin128, sin_signed = _rope_tables(seq)
    ones_c = jnp.ones((ctx_len, LANE), F32)
    zeros_c = jnp.zeros((ctx_len, LANE), F32)
    x2d = x.reshape(n, D)
    c2d = ctx.reshape(nc, D)

    mods = _modulation(cond, ev_mod_w[0].astype(BF16), ev_mod_b[0])
    sh1, sc1, g1, sh2, sc2, g2 = [m.reshape(mod_rows, 1, D) for m in jnp.split(mods, N_MOD, axis=-1)]
    ng = ev_norm_g[0].reshape(4, 1, D)
    w_in_e, wq_main, wq_swap, wkv = _even_weights(ev_w_in[0], ev_w_uq[0], ev_w_ukv[0])
    qg = ev_q_norm_g[0].reshape(1, MLA_QR)
    kvg = ev_kv_norm_g[0].reshape(1, MLA_KVR)
    w_out_e = ev_w_out[0].astype(BF16)
    wg_e, wu_e, wd_e = ev_ffn_gate[0].astype(BF16), ev_ffn_up[0].astype(BF16), ev_ffn_down[0].astype(BF16)

    bg_l, uc_l, q_l, k_l, v_l = _front_even(x2d, sh1, sc1, lat_row, ng[0], w_in_e, qg, wq_main, wq_swap, kvg, wkv,
                                            cos128, sin128, seq)
    bg_c, uc_c, q_c, k_c, v_c = _front_even(c2d, sh1, sc1, ctx_row, ng[0], w_in_e, qg, wq_main, wq_swap, kvg, wkv,
                                            ones_c, zeros_c, ctx_len)
    at_l = _mla_attn(q_l, k_c, v_c, k_l, v_l, nb, seq, ctx_len)
    at_c = _mla_attn(q_c, k_c, v_c, None, None, nb, ctx_len, ctx_len)
    x1_l, h2_l = _mix_even(bg_l, uc_l, ev_conv_w[0], at_l, w_out_e, x2d, g1, ng[1], ng[2], sh2, sc2, lat_row, seq)
    x1_c, h2_c = _mix_even(bg_c, uc_c, ev_conv_w[0], at_c, w_out_e, c2d, g1, ng[1], ng[2], sh2, sc2, ctx_row,
                           ctx_len)
    x2d = _ffn_dense(h2_l, wg_e, wu_e, wd_e, x1_l, g2, ng[3], lat_row, seq)
    c2d = _ffn_dense(h2_c, wg_e, wu_e, wd_e, x1_c, g2, ng[3], ctx_row, ctx_len)

    lam_init = 0.8 - 0.6 * math.exp(-0.3 * 1)
    mods = _modulation(cond, od_mod_w[0].astype(BF16), od_mod_b[0])
    sh1, sc1, g1, sh2, sc2, g2 = [m.reshape(mod_rows, 1, D) for m in jnp.split(mods, N_MOD, axis=-1)]
    ng = od_norm_g[0].reshape(4, 1, D)
    w_in_o = od_w_in[0].astype(BF16)
    q_o, k_o, v_o, f_o = _front_odd(x2d, sh1, sc1, lat_row, ng[0], w_in_o, cos128, sin_signed, seq, True)
    kc_o, vc_o = _front_odd(c2d, sh1, sc1, ctx_row, ng[0], w_in_o[:, DIFF_W:3 * DIFF_W], ones_c, zeros_c, ctx_len,
                            False)
    ca = _diff_attn(q_o, kc_o, vc_o, k_o, v_o, od_lambda[0], od_subln_g[0].reshape(1, 2 * DIFF_HD), nb, seq,
                    ctx_len, lam_init)
    ct, st, cc, sc = _dft_tables(seq)
    fd = _fourier(f_o, ct, st, cc, sc, nb, seq)
    router_f = jnp.zeros((D, LANE), F32).at[:, :N_EXP].set(od_router[0])
    router_hi = router_f.astype(BF16)
    router_pad = jnp.concatenate([router_hi, (router_f - router_hi.astype(F32)).astype(BF16)], axis=1)
    x1, h2p, ridx, rwt = _mix_odd(ca, fd, od_w_out[0].astype(BF16), x2d, g1, ng[1], ng[2], sh2, sc2, router_pad,
                                  lat_row, seq)
    ids, tile_expert, tile_valid = _route(ridx, n, TM_EXP)
    ys = _moe_experts(tile_expert, tile_valid, ids, h2p, od_exp_gate[0].astype(BF16), od_exp_up[0].astype(BF16),
                      od_exp_down[0].astype(BF16))
    out = _moe_combine(ys, rwt, x1, g2, ng[3], lat_row, seq)
    return out.reshape(nb, seq, D)
```
